```python
import jax, jax.numpy as jnp
from jax import lax
import numpy as np

D_MODEL = 1024
BATCH = 4
SEQ = 8192
DEPTH = 2

N_HEADS = 16
HEAD_DIM = D_MODEL // N_HEADS
N_A_LAYERS = DEPTH // 2
N_B_LAYERS = DEPTH - N_A_LAYERS
SB_Q_BLOCK = 128
MOBA_BLOCK = 256
MOBA_TOPK = 3
MOBA_Q_CHUNK = 64
N_GROUPS = 4
EXPERTS_PER_GROUP = 4
N_EXPERTS = N_GROUPS * EXPERTS_PER_GROUP
TOPK_IN_GROUP = 2
EXPERT_FF = 256
RMS_EPS = 1e-6
ALIBI_MAX_BIAS = 8.0

kernel_name = 'hybrid_yoco_stickbreak_moba_hmoe'


def rmsnorm(x, g):
    xf = x.astype(jnp.float32)
    y = xf * lax.rsqrt(jnp.mean(xf * xf, axis=-1, keepdims=True) + RMS_EPS)
    return (y * g.astype(jnp.float32)).astype(x.dtype)


def modulate(h, shift, scale):
    return h * (1.0 + scale[:, None, :]) + shift[:, None, :]


def split_heads(t):
    b, s, _ = t.shape
    return t.reshape(b, s, N_HEADS, HEAD_DIM).transpose(0, 2, 1, 3)


def merge_heads(t):
    b, h, s, d = t.shape
    return t.transpose(0, 2, 1, 3).reshape(b, s, h * d)


def alibi_slopes():
    return jnp.exp2(-ALIBI_MAX_BIAS * jnp.arange(1, N_HEADS + 1, dtype=jnp.float32) / N_HEADS)


def stick_breaking_attention(q, k, v):
    b, h, s, dh = q.shape
    nq = s // SB_Q_BLOCK
    scale = dh ** -0.5
    kpos = jnp.arange(s)
    qb = q.reshape(b, h, nq, SB_Q_BLOCK, dh).transpose(2, 0, 1, 3, 4)

    def block(args):
        qblk, i = args
        qpos = i * SB_Q_BLOCK + jnp.arange(SB_Q_BLOCK)
        z = jnp.einsum('bhqd,bhkd->bhqk', qblk, k).astype(jnp.float32) * scale
        past = kpos[None, :] < qpos[:, None]
        log_keep = jnp.where(past, jax.nn.log_sigmoid(-z), 0.0)
        suffix = lax.cumsum(log_keep, axis=3, reverse=True) - log_keep
        w = jnp.where(past, jnp.exp(jax.nn.log_sigmoid(z) + suffix), 0.0)
        return jnp.einsum('bhqk,bhkd->bhqd', w.astype(v.dtype), v)

    out = lax.map(block, (qb, jnp.arange(nq)))
    return out.transpose(1, 2, 0, 3, 4).reshape(b, h, s, dh)


def moba_attention(q, k_blocks, v_blocks, k_means, slopes):
    b, h, s, dh = q.shape
    nb = k_blocks.shape[2]
    topk = min(MOBA_TOPK, nb)
    nc = s // MOBA_Q_CHUNK
    scale = dh ** -0.5
    qc = q.reshape(b, h, nc, MOBA_Q_CHUNK, dh).transpose(2, 0, 1, 3, 4)
    blk_ids = jnp.arange(nb)
    in_blk = jnp.arange(MOBA_BLOCK)
    gather = jax.vmap(jax.vmap(lambda blocks, idx: blocks[idx]))

    def chunk(args):
        qblk, i = args
        qpos = i * MOBA_Q_CHUNK + jnp.arange(MOBA_Q_CHUNK)
        own = (i * MOBA_Q_CHUNK) // MOBA_BLOCK
        gate = jnp.einsum('bhqd,bhnd->bhqn', qblk, k_means).astype(jnp.float32)
        gate = jnp.where(blk_ids < own, gate, -jnp.inf)
        _, gidx = lax.top_k(gate, topk)
        valid = gidx < own
        ks = gather(k_blocks, gidx)
        vs = gather(v_blocks, gidx)
        ls = jnp.einsum('bhqd,bhqnkd->bhqnk', qblk, ks).astype(jnp.float32) * scale
        kpos_sel = gidx[..., None] * MOBA_BLOCK + in_blk
        dist_sel = (qpos[:, None, None] - kpos_sel).astype(jnp.float32)
        ls = ls - slopes[None, :, None, None, None] * dist_sel
        ls = jnp.where(valid[..., None], ls, -jnp.inf).reshape(b, h, MOBA_Q_CHUNK, topk * MOBA_BLOCK)
        k_own = lax.dynamic_index_in_dim(k_blocks, own, axis=2, keepdims=False)
        v_own = lax.dynamic_index_in_dim(v_blocks, own, axis=2, keepdims=False)
        lo = jnp.einsum('bhqd,bhkd->bhqk', qblk, k_own).astype(jnp.float32) * scale
        dist_own = qpos[:, None] - (own * MOBA_BLOCK + in_blk)[None, :]
        lo = jnp.where(dist_own >= 0,
                       lo - slopes[None, :, None, None] * dist_own.astype(jnp.float32),
                       -jnp.inf)
        p = jax.nn.softmax(jnp.concatenate([ls, lo], axis=-1), axis=-1)
        p_sel = p[..., :topk * MOBA_BLOCK].reshape(b, h, MOBA_Q_CHUNK, topk, MOBA_BLOCK)
        p_own = p[..., topk * MOBA_BLOCK:]
        return (jnp.einsum('bhqnk,bhqnkd->bhqd', p_sel.astype(vs.dtype), vs)
                + jnp.einsum('bhqk,bhkd->bhqd', p_own.astype(v_own.dtype), v_own))

    out = lax.map(chunk, (qc, jnp.arange(nc)))
    return out.transpose(1, 2, 0, 3, 4).reshape(b, h, s, dh)


def shared_kv(x, c, w_ada_kv, b_ada_kv, g_kv, w_kv):
    b, s, d = x.shape
    shift, scale = jnp.split(c @ w_ada_kv + b_ada_kv, 2, axis=-1)
    hkv = modulate(rmsnorm(x, g_kv), shift, scale)
    kv = hkv @ w_kv
    k = split_heads(kv[..., :d])
    v = split_heads(kv[..., d:])
    n_blocks = -(-s // MOBA_BLOCK)
    pad = n_blocks * MOBA_BLOCK - s
    k = jnp.pad(k, ((0, 0), (0, 0), (0, pad), (0, 0)))
    v = jnp.pad(v, ((0, 0), (0, 0), (0, pad), (0, 0)))
    k_blocks = k.reshape(b, N_HEADS, n_blocks, MOBA_BLOCK, HEAD_DIM)
    v_blocks = v.reshape(b, N_HEADS, n_blocks, MOBA_BLOCK, HEAD_DIM)
    k_means = jnp.mean(k_blocks.astype(jnp.float32), axis=3).astype(k.dtype)
    return k_blocks, v_blocks, k_means


def hier_moe(h, w_rg, b_rg, w_re, b_re, w_gate, w_up, w_down):
    b, s, d = h.shape
    t = h.reshape(b * s, d)
    n_tok = t.shape[0]
    tok = jnp.arange(n_tok)
    lg = (t @ w_rg + b_rg).astype(jnp.float32)
    pg = jax.nn.softmax(lg, axis=-1)
    g_sel = jnp.argmax(lg, axis=-1)
    p_g = pg[tok, g_sel][:, None]
    le = (t @ w_re + b_re).astype(jnp.float32).reshape(n_tok, N_GROUPS, EXPERTS_PER_GROUP)
    le_g = le[tok, g_sel]
    ve, ie = lax.top_k(le_g, TOPK_IN_GROUP)
    pe = jax.nn.softmax(ve, axis=-1) * p_g
    eid = g_sel[:, None] * EXPERTS_PER_GROUP + ie
    combine = jnp.sum(jax.nn.one_hot(eid, N_EXPERTS, dtype=jnp.float32) * pe[..., None], axis=1)
    hg = jnp.einsum('td,edf->tef', t, w_gate)
    hu = jnp.einsum('td,edf->tef', t, w_up)
    hid = jax.nn.silu(hg) * hu * combine[:, :, None].astype(t.dtype)
    out = jnp.einsum('tef,efd->td', hid, w_down)
    return out.reshape(b, s, d)


def setup_inputs(seed: int = 0) -> dict:
    key = jax.random.key(seed)
    ks = jax.random.split(key, 24)
    D = D_MODEL
    nA = max(N_A_LAYERS, 1)
    nB = max(N_B_LAYERS, 1)

    def nrm(k, shape, fan_in, gain=1.0):
        return jax.random.normal(k, shape, jnp.float32) * (gain * fan_in ** -0.5)

    def gain(k, shape):
        return 1.0 + 0.02 * jax.random.normal(k, shape, jnp.float32)

    def small(k, shape, s=0.01):
        return s * jax.random.normal(k, shape, jnp.float32)

    return {
        'x': jax.random.normal(ks[0], (BATCH, SEQ, D), jnp.float32),
        'c': jax.random.normal(ks[1], (BATCH, D), jnp.float32),
        'w_ada': nrm(ks[2], (DEPTH, D, 6 * D), D, 0.1),
        'b_ada': small(ks[3], (DEPTH, 6 * D)),
        'g_attn': gain(ks[4], (DEPTH, D)),
        'g_ffn': gain(ks[5], (DEPTH, D)),
        'w_qkv_a': nrm(ks[6], (nA, D, 3 * D), D),
        'w_o_a': nrm(ks[7], (nA, D, D), D),
        'w_ada_kv': nrm(ks[8], (D, 2 * D), D, 0.1),
        'b_ada_kv': small(ks[9], (2 * D,)),
        'g_kv': gain(ks[10], (D,)),
        'w_kv': nrm(ks[11], (D, 2 * D), D),
        'w_q_b': nrm(ks[12], (nB, D, D), D),
        'w_o_b': nrm(ks[13], (nB, D, D), D),
        'w_rg': nrm(ks[14], (DEPTH, D, N_GROUPS), D),
        'b_rg': small(ks[15], (DEPTH, N_GROUPS)),
        'w_re': nrm(ks[16], (DEPTH, D, N_EXPERTS), D),
        'b_re': small(ks[17], (DEPTH, N_EXPERTS)),
        'w_gate': nrm(ks[18], (DEPTH, N_EXPERTS, D, EXPERT_FF), D),
        'w_up': nrm(ks[19], (DEPTH, N_EXPERTS, D, EXPERT_FF), D),
        'w_down': nrm(ks[20], (DEPTH, N_EXPERTS, EXPERT_FF, D), EXPERT_FF),
        'g_final': gain(ks[21], (D,)),
    }


def reference(x, c, w_ada, b_ada, g_attn, g_ffn, w_qkv_a, w_o_a, w_ada_kv, b_ada_kv, g_kv, w_kv,
              w_q_b, w_o_b, w_rg, b_rg, w_re, b_re, w_gate, w_up, w_down, g_final):
    b, s, d = x.shape
    slopes = alibi_slopes()
    k_blocks = v_blocks = k_means = None
    for l in range(DEPTH):
        mod = c @ w_ada[l] + b_ada[l]
        sh_m, sc_m, gt_m, sh_f, sc_f, gt_f = jnp.split(mod, 6, axis=-1)
        h = modulate(rmsnorm(x, g_attn[l]), sh_m, sc_m)
        if l < N_A_LAYERS:
            qkv = (h @ w_qkv_a[l]).reshape(b, s, 3, N_HEADS, HEAD_DIM).transpose(2, 0, 3, 1, 4)
            o = merge_heads(stick_breaking_attention(qkv[0], qkv[1], qkv[2])) @ w_o_a[l]
        else:
            j = l - N_A_LAYERS
            q = split_heads(h @ w_q_b[j])
            o = merge_heads(moba_attention(q, k_blocks, v_blocks, k_means, slopes)) @ w_o_b[j]
        x = x + (1.0 + gt_m)[:, None, :] * o
        h = modulate(rmsnorm(x, g_ffn[l]), sh_f, sc_f)
        x = x + (1.0 + gt_f)[:, None, :] * hier_moe(h, w_rg[l], b_rg[l], w_re[l], b_re[l],
                                                   w_gate[l], w_up[l], w_down[l])
        if l == N_A_LAYERS - 1:
            k_blocks, v_blocks, k_means = shared_kv(x, c, w_ada_kv, b_ada_kv, g_kv, w_kv)
    return rmsnorm(x, g_final)
```

```python
import functools

import jax
import jax.numpy as jnp
from jax import lax
from jax.experimental import pallas as pl
from jax.experimental.pallas import tpu as pltpu

N_HEADS = 16
MOBA_BLOCK = 256
MOBA_TOPK = 3
N_GROUPS = 4
EXPERTS_PER_GROUP = 4
N_EXPERTS = N_GROUPS * EXPERTS_PER_GROUP
RMS_EPS = 1e-6
ALIBI_MAX_BIAS = 8.0

LANES = 128
VMEM_LIMIT = 56 * 1024 * 1024
MASKED = -1e30

F32 = jnp.float32
BF16 = jnp.bfloat16
NT_DIMS = (((1,), (1,)), ((), ()))


def _params(*sem):
    return pltpu.CompilerParams(dimension_semantics=sem, vmem_limit_bytes=VMEM_LIMIT)


def _ada_body(c_ref, w_ref, b_ref, o_ref):
    o_ref[0] = jnp.dot(c_ref[...], w_ref[0], preferred_element_type=F32,
                       precision=lax.Precision.HIGHEST) + b_ref[0]


def _ada(c, w, b, tn=512):
    nl, d, n = w.shape
    bsz = c.shape[0]
    return pl.pallas_call(
        _ada_body,
        grid=(nl, n // tn),
        in_specs=[pl.BlockSpec((bsz, d), lambda l, j: (0, 0)),
                  pl.BlockSpec((1, d, tn), lambda l, j: (l, 0, j)),
                  pl.BlockSpec((1, 1, tn), lambda l, j: (l, 0, j))],
        out_specs=pl.BlockSpec((1, bsz, tn), lambda l, j: (l, 0, j)),
        out_shape=jax.ShapeDtypeStruct((nl, bsz, n), F32),
        compiler_params=_params("parallel", "parallel"),
        name="ada_mod",
    )(c, w, b.reshape(nl, 1, n))


def _norm_mod(x, g, shift, scale):
    ms = jnp.mean(x * x, axis=-1, keepdims=True)
    y = x * lax.rsqrt(ms + RMS_EPS) * g
    return y * (1.0 + scale) + shift


def _norm_mm_body(x_ref, g_ref, sh_ref, sc_ref, w_ref, o_ref, *rest, d_model, with_means):
    h = _norm_mod(x_ref[0], g_ref[...], sh_ref[0], sc_ref[0])
    y = jnp.dot(h.astype(BF16), w_ref[...], preferred_element_type=F32)
    o_ref[0] = y.astype(o_ref.dtype)
    if with_means:
        km_ref, = rest
        tm = y.shape[0]
        for r in range(tm // MOBA_BLOCK):
            blk = y[r * MOBA_BLOCK:(r + 1) * MOBA_BLOCK, :d_model]
            km_ref[0, r] = jnp.mean(blk, axis=0, keepdims=True)


def _norm_matmul(x, g, mod, shift_idx, scale_idx, w, with_means=False, tm=512):
    b, s, d = x.shape
    n = w.shape[1]
    out_shape = [jax.ShapeDtypeStruct((b, s, n), BF16)]
    out_specs = [pl.BlockSpec((1, tm, n), lambda bi, i: (bi, i, 0))]
    if with_means:
        out_shape.append(jax.ShapeDtypeStruct((b, s // MOBA_BLOCK, 1, d), F32))
        out_specs.append(pl.BlockSpec((1, tm // MOBA_BLOCK, 1, d), lambda bi, i: (bi, i, 0, 0)))
    res = pl.pallas_call(
        functools.partial(_norm_mm_body, d_model=d, with_means=with_means),
        grid=(b, s // tm),
        in_specs=[pl.BlockSpec((1, tm, d), lambda bi, i: (bi, i, 0)),
                  pl.BlockSpec((1, d), lambda bi, i: (0, 0)),
                  pl.BlockSpec((1, 1, d), lambda bi, i: (bi, 0, shift_idx)),
                  pl.BlockSpec((1, 1, d), lambda bi, i: (bi, 0, scale_idx)),
                  pl.BlockSpec((d, n), lambda bi, i: (0, 0))],
        out_specs=out_specs,
        out_shape=out_shape,
        compiler_params=_params("parallel", "parallel"),
        name="norm_matmul_means" if with_means else "norm_matmul",
    )(x, g.reshape(1, d), mod, mod, w)
    return res if with_means else res[0]


def _sb_body(q_ref, k_ref, v_ref, o_ref, acc_ref, *, tq, dh):
    i = pl.program_id(2)
    lane = lax.broadcasted_iota(jnp.int32, (tq, LANES), 1)
    q2 = q_ref[0] * jnp.asarray(dh ** -0.5, BF16)
    zero = jnp.zeros_like(q2)
    qh = (jnp.where(lane < dh, q2, zero), jnp.where(lane >= dh, q2, zero))
    row = lax.broadcasted_iota(jnp.int32, (tq, tq), 0)
    col = lax.broadcasted_iota(jnp.int32, (tq, tq), 1)
    past = col < row
    neg_tri = jnp.where(row > col, -1.0, 0.0).astype(BF16)

    acc_ref[...] = jnp.zeros_like(acc_ref)

    def block(j, carries, diag):
        start = pl.multiple_of(j * tq, tq)
        k2 = k_ref[0, pl.ds(start, tq), :]
        v2 = v_ref[0, pl.ds(start, tq), :]
        out = []
        for h in range(2):
            z = lax.dot_general(qh[h], k2, NT_DIMS, preferred_element_type=F32)
            sp = jnp.maximum(z, 0.0) + jnp.log(1.0 + jnp.exp(-jnp.abs(z)))
            if diag:
                sp = jnp.where(past, sp, 0.0)
            suffix = jnp.dot(sp.astype(BF16), neg_tri, preferred_element_type=F32) + carries[h]
            w = jnp.exp(z - sp + suffix)
            if diag:
                w = jnp.where(past, w, 0.0)
            acc_ref[h] += jnp.dot(w.astype(BF16), v2, preferred_element_type=F32)
            out.append(carries[h] - jnp.sum(sp, axis=-1, keepdims=True))
        return tuple(out)

    zeros = jnp.zeros((tq, 1), F32)
    carries = block(i, (zeros, zeros), True)
    lax.fori_loop(0, i, lambda t, c: block(i - 1 - t, c, False), carries)
    o_ref[0] = jnp.where(lane < dh, acc_ref[0], acc_ref[1]).astype(o_ref.dtype)


def _sb_attention(qkv, d, tq=256):
    b, s, _ = qkv.shape
    dh = d // N_HEADS
    assert 2 * dh == LANES and s % tq == 0
    npair = d // LANES
    return pl.pallas_call(
        functools.partial(_sb_body, tq=tq, dh=dh),
        grid=(b, npair, s // tq),
        in_specs=[pl.BlockSpec((1, tq, LANES), lambda bi, hp, i: (bi, i, hp)),
                  pl.BlockSpec((1, s, LANES), lambda bi, hp, i: (bi, 0, npair + hp)),
                  pl.BlockSpec((1, s, LANES), lambda bi, hp, i: (bi, 0, 2 * npair + hp))],
        out_specs=pl.BlockSpec((1, tq, LANES), lambda bi, hp, i: (bi, i, hp)),
        out_shape=jax.ShapeDtypeStruct((b, s, d), BF16),
        scratch_shapes=[pltpu.VMEM((2, tq, LANES), F32)],
        compiler_params=_params("parallel", "parallel", "arbitrary"),
        name="sb_attention",
    )(qkv, qkv, qkv)


def _moba_body(slopes_ref, q_ref, k_ref, v_ref, km_ref, o_ref,
               acc_ref, m_ref, l_ref, bias_ref, sel_ref, *, tq, dh, nb, topk):
    hp = pl.program_id(1)
    i = pl.program_id(2)
    lane = lax.broadcasted_iota(jnp.int32, (tq, LANES), 1)
    q2 = q_ref[0]
    zero = jnp.zeros_like(q2)
    qh = (jnp.where(lane < dh, q2, zero), jnp.where(lane >= dh, q2, zero))
    km = km_ref[0]
    row = lax.broadcasted_iota(jnp.int32, (tq, tq), 0)
    col = lax.broadcasted_iota(jnp.int32, (tq, tq), 1)
    rel = (row - col).astype(F32)
    blk = lax.broadcasted_iota(jnp.int32, (tq, nb), 1).astype(F32)
    own = i.astype(F32)
    scale = jnp.asarray(dh ** -0.5, BF16)
    start_own = pl.multiple_of(i * tq, tq)
    k_own = k_ref[0, pl.ds(start_own, tq), :]
    v_own = v_ref[0, pl.ds(start_own, tq), :]

    for h in range(2):
        slope = slopes_ref[2 * hp + h]
        gate = lax.dot_general(qh[h].astype(F32), km, NT_DIMS, preferred_element_type=F32,
                               precision=lax.Precision.HIGHEST)
        gate = jnp.where(blk < own, gate, -jnp.inf)
        sel = jnp.zeros((tq, nb), F32)
        for _ in range(topk):
            mx = jnp.max(gate, axis=-1, keepdims=True)
            idx = jnp.min(jnp.where(gate == mx, blk, float(nb)), axis=-1, keepdims=True)
            pick = blk == idx
            sel = jnp.where(pick & (blk < own), 1.0, sel)
            gate = jnp.where(pick, -jnp.inf, gate)
        sel_ref[h] = sel
        bias_ref[h] = -slope * rel
        z = lax.dot_general(qh[h] * scale, k_own, NT_DIMS, preferred_element_type=F32)
        s = jnp.where(rel >= 0.0, z + bias_ref[h], MASKED)
        m = jnp.max(s, axis=-1, keepdims=True)
        p = jnp.exp(s - m)
        m_ref[h] = m
        l_ref[h] = jnp.sum(p, axis=-1, keepdims=True)
        acc_ref[h] = jnp.dot(p.astype(BF16), v_own, preferred_element_type=F32)

    def body(j, carry):
        start = pl.multiple_of(j * tq, tq)
        k2 = k_ref[0, pl.ds(start, tq), :]
        v2 = v_ref[0, pl.ds(start, tq), :]
        jf = j.astype(F32)
        for h in range(2):
            slope = slopes_ref[2 * hp + h]
            picked = jnp.sum(jnp.where(blk == jf, sel_ref[h], 0.0), axis=-1, keepdims=True)
            off = jnp.where(picked > 0.0, -slope * (own - jf) * float(tq), MASKED)
            z = lax.dot_general(qh[h] * scale, k2, NT_DIMS, preferred_element_type=F32)
            s = z + bias_ref[h] + off
            m_old = m_ref[h]
            m_new = jnp.maximum(m_old, jnp.max(s, axis=-1, keepdims=True))
            alpha = jnp.exp(m_old - m_new)
            p = jnp.exp(s - m_new)
            m_ref[h] = m_new
            l_ref[h] = alpha * l_ref[h] + jnp.sum(p, axis=-1, keepdims=True)
            acc_ref[h] = alpha * acc_ref[h] + jnp.dot(p.astype(BF16), v2, preferred_element_type=F32)
        return carry

    lax.fori_loop(0, i, body, 0)
    o0 = acc_ref[0] / l_ref[0]
    o1 = acc_ref[1] / l_ref[1]
    o_ref[0] = jnp.where(lane < dh, o0, o1).astype(o_ref.dtype)


def _moba_attention(q, kv, kmeans, slopes):
    b, s, d = q.shape
    dh = d // N_HEADS
    tq = MOBA_BLOCK
    assert 2 * dh == LANES and s % tq == 0
    nb = s // tq
    npair = d // LANES
    return pl.pallas_call(
        functools.partial(_moba_body, tq=tq, dh=dh, nb=nb, topk=min(MOBA_TOPK, nb)),
        grid=(b, npair, nb),
        in_specs=[pl.BlockSpec(memory_space=pltpu.SMEM),
                  pl.BlockSpec((1, tq, LANES), lambda bi, hp, i: (bi, i, hp)),
                  pl.BlockSpec((1, s, LANES), lambda bi, hp, i: (bi, 0, hp)),
                  pl.BlockSpec((1, s, LANES), lambda bi, hp, i: (bi, 0, npair + hp)),
                  pl.BlockSpec((1, nb, LANES), lambda bi, hp, i: (bi, 0, hp))],
        out_specs=pl.BlockSpec((1, tq, LANES), lambda bi, hp, i: (bi, i, hp)),
        out_shape=jax.ShapeDtypeStruct((b, s, d), BF16),
        scratch_shapes=[pltpu.VMEM((2, tq, LANES), F32),
                        pltpu.VMEM((2, tq, 1), F32),
                        pltpu.VMEM((2, tq, 1), F32),
                        pltpu.VMEM((2, tq, tq), F32),
                        pltpu.VMEM((2, tq, nb), F32)],
        compiler_params=_params("parallel", "parallel", "arbitrary"),
        name="moba_attention",
    )(slopes, q, kv, kv, kmeans)


def _proj_res_body(o_ref, w_ref, x_ref, gt_ref, out_ref):
    y = jnp.dot(o_ref[0], w_ref[...], preferred_element_type=F32)
    out_ref[0] = x_ref[0] + (1.0 + gt_ref[0]) * y


def _proj_residual(o, w, x, mod, gate_idx, tm=512):
    b, s, d = x.shape
    return pl.pallas_call(
        _proj_res_body,
        grid=(b, s // tm),
        in_specs=[pl.BlockSpec((1, tm, d), lambda bi, i: (bi, i, 0)),
                  pl.BlockSpec((d, d), lambda bi, i: (0, 0)),
                  pl.BlockSpec((1, tm, d), lambda bi, i: (bi, i, 0)),
                  pl.BlockSpec((1, 1, d), lambda bi, i: (bi, 0, gate_idx))],
        out_specs=pl.BlockSpec((1, tm, d), lambda bi, i: (bi, i, 0)),
        out_shape=jax.ShapeDtypeStruct((b, s, d), F32),
        compiler_params=_params("parallel", "parallel"),
        name="proj_residual",
    )(o, w, x, mod)


def _route(logits):
    tm = logits.shape[0]
    lane = lax.broadcasted_iota(jnp.int32, (tm, LANES), 1).astype(F32)
    big = float(LANES)
    is_grp = (lane >= float(N_EXPERTS)) & (lane < float(N_EXPERTS + N_GROUPS))
    lg = jnp.where(is_grp, logits, -jnp.inf)
    mg = jnp.max(lg, axis=-1, keepdims=True)
    p_g = 1.0 / jnp.sum(jnp.exp(lg - mg), axis=-1, keepdims=True)
    g_sel = jnp.min(jnp.where(lg == mg, lane, big), axis=-1, keepdims=True) - float(N_EXPERTS)
    lo = g_sel * float(EXPERTS_PER_GROUP)
    in_grp = (lane >= lo) & (lane < lo + float(EXPERTS_PER_GROUP))
    le = jnp.where(in_grp, logits, -jnp.inf)
    m1 = jnp.max(le, axis=-1, keepdims=True)
    i1 = jnp.min(jnp.where(le == m1, lane, big), axis=-1, keepdims=True)
    le2 = jnp.where(lane == i1, -jnp.inf, le)
    m2 = jnp.max(le2, axis=-1, keepdims=True)
    i2 = jnp.min(jnp.where(le2 == m2, lane, big), axis=-1, keepdims=True)
    e2 = jnp.exp(m2 - m1)
    den = 1.0 + e2
    return jnp.where(lane == i1, p_g / den, 0.0) + jnp.where(lane == i2, p_g * e2 / den, 0.0)


def _moe_body(x_ref, g_ref, sh_ref, sc_ref, gt_ref, wr_ref, br_ref, wg_ref, wu_ref, wd_ref, gf_ref,
              out_ref, h_ref, comb_ref, acc_ref, *, final_norm):
    gi = pl.program_id(1)

    @pl.when(gi == 0)
    def _():
        h = _norm_mod(x_ref[...], g_ref[...], sh_ref[0], sc_ref[0])
        h_ref[...] = h.astype(BF16)
        logits = jnp.dot(h, wr_ref[...], preferred_element_type=F32,
                         precision=lax.Precision.HIGHEST) + br_ref[...]
        comb_ref[...] = _route(logits)
        acc_ref[...] = jnp.zeros_like(acc_ref)

    h = h_ref[...]
    comb = comb_ref[...]
    lane = lax.broadcasted_iota(jnp.int32, comb.shape, 1)
    for e in range(EXPERTS_PER_GROUP):
        ce = jnp.sum(jnp.where(lane == gi * EXPERTS_PER_GROUP + e, comb, 0.0), axis=-1, keepdims=True)
        hg = jnp.dot(h, wg_ref[e], preferred_element_type=F32)
        hu = jnp.dot(h, wu_ref[e], preferred_element_type=F32)
        hid = hg * (1.0 / (1.0 + jnp.exp(-hg))) * hu * ce
        acc_ref[...] += jnp.dot(hid.astype(BF16), wd_ref[e], preferred_element_type=F32)

    @pl.when(gi == N_GROUPS - 1)
    def _():
        y = x_ref[...] + (1.0 + gt_ref[0]) * acc_ref[...]
        if final_norm:
            ms = jnp.mean(y * y, axis=-1, keepdims=True)
            y = y * lax.rsqrt(ms + RMS_EPS) * gf_ref[...]
        out_ref[...] = y


def _moe(x, g, mod, shift_idx, scale_idx, gate_idx, w_router, b_router, w_gate, w_up, w_down, g_final,
         final_norm, tm=512):
    b, s, d = x.shape
    ff = w_gate.shape[2]
    t = b * s
    per_batch = s // tm
    epg = EXPERTS_PER_GROUP
    out = pl.pallas_call(
        functools.partial(_moe_body, final_norm=final_norm),
        grid=(t // tm, N_GROUPS),
        in_specs=[pl.BlockSpec((tm, d), lambda i, gi: (i, 0)),
                  pl.BlockSpec((1, d), lambda i, gi: (0, 0)),
                  pl.BlockSpec((1, 1, d), lambda i, gi: (i // per_batch, 0, shift_idx)),
                  pl.BlockSpec((1, 1, d), lambda i, gi: (i // per_batch, 0, scale_idx)),
                  pl.BlockSpec((1, 1, d), lambda i, gi: (i // per_batch, 0, gate_idx)),
                  pl.BlockSpec((d, LANES), lambda i, gi: (0, 0)),
                  pl.BlockSpec((1, LANES), lambda i, gi: (0, 0)),
                  pl.BlockSpec((epg, d, ff), lambda i, gi: (gi, 0, 0)),
                  pl.BlockSpec((epg, d, ff), lambda i, gi: (gi, 0, 0)),
                  pl.BlockSpec((epg, ff, d), lambda i, gi: (gi, 0, 0)),
                  pl.BlockSpec((1, d), lambda i, gi: (0, 0))],
        out_specs=pl.BlockSpec((tm, d), lambda i, gi: (i, 0)),
        out_shape=jax.ShapeDtypeStruct((t, d), F32),
        scratch_shapes=[pltpu.VMEM((tm, d), BF16),
                        pltpu.VMEM((tm, LANES), F32),
                        pltpu.VMEM((tm, d), F32)],
        compiler_params=_params("parallel", "arbitrary"),
        name="moe_final" if final_norm else "moe",
    )(x.reshape(t, d), g.reshape(1, d), mod, mod, mod, w_router, b_router, w_gate, w_up, w_down,
      g_final.reshape(1, d))
    return out.reshape(b, s, d)


def _router_params(w_rg, b_rg, w_re, b_re):
    d = w_rg.shape[0]
    pad = LANES - N_EXPERTS - N_GROUPS
    w = jnp.concatenate([w_re, w_rg, jnp.zeros((d, pad), F32)], axis=1)
    bias = jnp.concatenate([b_re, b_rg, jnp.zeros((pad,), F32)]).reshape(1, LANES)
    return w, bias


def kernel(x, c, w_ada, b_ada, g_attn, g_ffn, w_qkv_a, w_o_a, w_ada_kv, b_ada_kv, g_kv, w_kv, w_q_b, w_o_b,
           w_rg, b_rg, w_re, b_re, w_gate, w_up, w_down, g_final):
    b, s, d = x.shape
    depth = w_ada.shape[0]
    n_a = depth // 2
    slopes = jnp.exp2(-ALIBI_MAX_BIAS * jnp.arange(1, N_HEADS + 1, dtype=F32) / N_HEADS)
    mods = _ada(c, w_ada, b_ada).reshape(depth, b, 1, 6 * d)
    mod_kv = _ada(c, w_ada_kv[None], b_ada_kv[None]).reshape(b, 1, 2 * d)
    kv = kmeans = None
    for l in range(depth):
        mod = mods[l]
        if l < n_a:
            qkv = _norm_matmul(x, g_attn[l], mod, 0, 1, w_qkv_a[l].astype(BF16))
            o = _sb_attention(qkv, d)
            w_o = w_o_a[l]
        else:
            j = l - n_a
            q = _norm_matmul(x, g_attn[l], mod, 0, 1, w_q_b[j].astype(BF16))
            o = _moba_attention(q, kv, kmeans, slopes)
            w_o = w_o_b[j]
        x = _proj_residual(o, w_o.astype(BF16), x, mod, 2)
        w_router, b_router = _router_params(w_rg[l], b_rg[l], w_re[l], b_re[l])
        x = _moe(x, g_ffn[l], mod, 3, 4, 5, w_router, b_router, w_gate[l].astype(BF16), w_up[l].astype(BF16),
                 w_down[l].astype(BF16), g_final, final_norm=(l == depth - 1))
        if l == n_a - 1:
            kv, kmeans = _norm_matmul(x, g_kv, mod_kv, 0, 1, w_kv.astype(BF16), with_means=True)
            kmeans = kmeans.reshape(b, s // MOBA_BLOCK, d)
    return x
```

```python
import functools
import math

import jax
import jax.numpy as jnp
from jax import lax
from jax.experimental import pallas as pl
from jax.experimental.pallas import tpu as pltpu

N_HEADS = 16
MOBA_BLOCK = 256
MOBA_TOPK = 3
N_GROUPS = 4
EXPERTS_PER_GROUP = 4
N_EXPERTS = N_GROUPS * EXPERTS_PER_GROUP
RMS_EPS = 1e-6
ALIBI_MAX_BIAS = 8.0

LANES = 128
VMEM_LIMIT = 56 * 1024 * 1024
MASKED = -1e30
LOG2E = math.log2(math.e)

F32 = jnp.float32
BF16 = jnp.bfloat16
NT_DIMS = (((1,), (1,)), ((), ()))


def _params(*sem):
    return pltpu.CompilerParams(dimension_semantics=sem, vmem_limit_bytes=VMEM_LIMIT)


def _ada_body(c_ref, w_ref, b_ref, o_ref):
    o_ref[0] = jnp.dot(c_ref[...], w_ref[0], preferred_element_type=F32,
                       precision=lax.Precision.HIGHEST) + b_ref[0]


def _ada(c, w, b, tn=512):
    nl, d, n = w.shape
    bsz = c.shape[0]
    return pl.pallas_call(
        _ada_body,
        grid=(nl, n // tn),
        in_specs=[pl.BlockSpec((bsz, d), lambda l, j: (0, 0)),
                  pl.BlockSpec((1, d, tn), lambda l, j: (l, 0, j)),
                  pl.BlockSpec((1, 1, tn), lambda l, j: (l, 0, j))],
        out_specs=pl.BlockSpec((1, bsz, tn), lambda l, j: (l, 0, j)),
        out_shape=jax.ShapeDtypeStruct((nl, bsz, n), F32),
        compiler_params=_params("parallel", "parallel"),
        name="ada_mod",
    )(c, w, b.reshape(nl, 1, n))


def _norm_mod(x, g, shift, scale):
    ms = jnp.mean(x * x, axis=-1, keepdims=True)
    y = x * lax.rsqrt(ms + RMS_EPS) * g
    return y * (1.0 + scale) + shift


def _norm_mm_body(x_ref, g_ref, sh_ref, sc_ref, w_ref, o_ref, *rest, n_plain, d_model, with_t, with_means):
    h = _norm_mod(x_ref[0], g_ref[...], sh_ref[0], sc_ref[0])
    y = jnp.dot(h.astype(BF16), w_ref[...], preferred_element_type=F32)
    o_ref[0] = y[:, :n_plain].astype(o_ref.dtype)
    if with_t:
        rest[0][0] = y[:, n_plain:].T.astype(BF16)
    if with_means:
        km_ref = rest[-1]
        for r in range(y.shape[0] // MOBA_BLOCK):
            blk = y[r * MOBA_BLOCK:(r + 1) * MOBA_BLOCK, :d_model]
            km_ref[0, r] = jnp.mean(blk, axis=0, keepdims=True)


def _norm_matmul(x, g, mod, shift_idx, scale_idx, w, with_t=False, with_means=False, tm=512):
    b, s, d = x.shape
    n = w.shape[1]
    n_plain = n - d if with_t else n
    out_shape = [jax.ShapeDtypeStruct((b, s, n_plain), BF16)]
    out_specs = [pl.BlockSpec((1, tm, n_plain), lambda bi, i: (bi, i, 0))]
    if with_t:
        out_shape.append(jax.ShapeDtypeStruct((b, d, s), BF16))
        out_specs.append(pl.BlockSpec((1, d, tm), lambda bi, i: (bi, 0, i)))
    if with_means:
        out_shape.append(jax.ShapeDtypeStruct((b, s // MOBA_BLOCK, 1, d), F32))
        out_specs.append(pl.BlockSpec((1, tm // MOBA_BLOCK, 1, d), lambda bi, i: (bi, i, 0, 0)))
    return pl.pallas_call(
        functools.partial(_norm_mm_body, n_plain=n_plain, d_model=d, with_t=with_t, with_means=with_means),
        grid=(b, s // tm),
        in_specs=[pl.BlockSpec((1, tm, d), lambda bi, i: (bi, i, 0)),
                  pl.BlockSpec((1, d), lambda bi, i: (0, 0)),
                  pl.BlockSpec((1, 1, d), lambda bi, i: (bi, 0, shift_idx)),
                  pl.BlockSpec((1, 1, d), lambda bi, i: (bi, 0, scale_idx)),
                  pl.BlockSpec((d, n), lambda bi, i: (0, 0))],
        out_specs=out_specs,
        out_shape=out_shape,
        compiler_params=_params("parallel", "parallel"),
        name="norm_matmul" + ("_t" if with_t else "") + ("_means" if with_means else ""),
    )(x, g.reshape(1, d), mod, mod, w)


def _head_pair(q2, dh):
    lane = lax.broadcasted_iota(jnp.int32, q2.shape, 1)
    zero = jnp.zeros_like(q2)
    return jnp.where(lane < dh, q2, zero), jnp.where(lane >= dh, q2, zero)


def _sb_body(q_ref, k_ref, vt_ref, o_ref, acc_ref, *, tq, dh, group_size):
    i = pl.program_id(2)
    qh = _head_pair(q_ref[0], dh)
    key = lax.broadcasted_iota(jnp.int32, (tq, tq), 0)
    qry = lax.broadcasted_iota(jnp.int32, (tq, tq), 1)
    past = key < qry
    neg_incl = jnp.where(qry >= key, -1.0, 0.0).astype(BF16)
    sign_bit = jnp.uint32(0x80000000)

    acc_ref[...] = jnp.zeros_like(acc_ref)

    def blocks(js, carries, diag=False):
        ks = [k_ref[0, pl.ds(pl.multiple_of(j * tq, tq), tq), :] for j in js]
        vts = [vt_ref[0, :, pl.ds(pl.multiple_of(j * tq, tq), tq)] for j in js]
        out = []
        for h in range(2):
            carry = carries[h]
            acc = acc_ref[h]
            for k2, vt in zip(ks, vts):
                z = lax.dot_general(k2, qh[h], NT_DIMS, preferred_element_type=F32)
                neg_abs = lax.bitcast_convert_type(lax.bitcast_convert_type(z, jnp.uint32) | sign_bit, F32)
                sp = jnp.maximum(z, 0.0) + jnp.log2(1.0 + jnp.exp2(neg_abs))
                if diag:
                    sp = jnp.where(past, sp, 0.0)
                incl = jnp.dot(neg_incl, sp.astype(BF16), preferred_element_type=F32)
                w = jnp.exp2(z + incl + carry)
                if diag:
                    w = jnp.where(past, w, 0.0)
                acc = acc + jnp.dot(vt, w.astype(BF16), preferred_element_type=F32)
                carry = carry + incl[0:1, :]
            acc_ref[h] = acc
            out.append(carry)
        return tuple(out)

    zeros = jnp.zeros((1, tq), F32)
    carries = blocks([i], (zeros, zeros), diag=True)
    rem = i % group_size
    carries = lax.fori_loop(0, rem, lambda t, c: blocks([i - 1 - t], c), carries)
    top = i - 1 - rem
    lax.fori_loop(0, i // group_size,
                  lambda t, c: blocks([top - group_size * t - g for g in range(group_size)], c), carries)
    row = lax.broadcasted_iota(jnp.int32, (LANES, tq), 0)
    o_ref[0] = jnp.where(row < dh, acc_ref[0], acc_ref[1]).T.astype(o_ref.dtype)


def _sb_attention(qk, vt, d, tq=256, group_size=4):
    b, s, _ = qk.shape
    dh = d // N_HEADS
    assert 2 * dh == LANES and s % tq == 0
    npair = d // LANES
    return pl.pallas_call(
        functools.partial(_sb_body, tq=tq, dh=dh, group_size=group_size),
        grid=(b, npair, s // tq),
        in_specs=[pl.BlockSpec((1, tq, LANES), lambda bi, hp, i: (bi, i, hp)),
                  pl.BlockSpec((1, s, LANES), lambda bi, hp, i: (bi, 0, npair + hp)),
                  pl.BlockSpec((1, LANES, s), lambda bi, hp, i: (bi, hp, 0))],
        out_specs=pl.BlockSpec((1, tq, LANES), lambda bi, hp, i: (bi, i, hp)),
        out_shape=jax.ShapeDtypeStruct((b, s, d), BF16),
        scratch_shapes=[pltpu.VMEM((2, LANES, tq), F32)],
        compiler_params=_params("parallel", "parallel", "arbitrary"),
        name="sb_attention",
    )(qk, qk, vt)


def _moba_body(slopes_ref, q_ref, k_ref, vt_ref, km_ref, o_ref,
               acc_ref, m_ref, l_ref, bias_ref, off_ref, *, tq, dh, nb, topk, group_size):
    hp = pl.program_id(1)
    i = pl.program_id(2)
    qh = _head_pair(q_ref[0], dh)
    km = km_ref[0]
    key = lax.broadcasted_iota(jnp.int32, (tq, tq), 0)
    qry = lax.broadcasted_iota(jnp.int32, (tq, tq), 1)
    rel = (qry - key).astype(F32)
    causal = qry >= key
    blk = lax.broadcasted_iota(jnp.int32, (nb, tq), 0).astype(F32)
    own = i.astype(F32)
    start_own = pl.multiple_of(i * tq, tq)
    k_own = k_ref[0, pl.ds(start_own, tq), :]
    vt_own = vt_ref[0, :, pl.ds(start_own, tq)]

    for h in range(2):
        slope = slopes_ref[2 * hp + h]
        gate = lax.dot_general(km, qh[h].astype(F32), NT_DIMS, preferred_element_type=F32,
                               precision=lax.Precision.HIGHEST)
        gate = jnp.where(blk < own, gate, -jnp.inf)
        sel = jnp.zeros((nb, tq), F32)
        for _ in range(topk):
            mx = jnp.max(gate, axis=0, keepdims=True)
            idx = jnp.min(jnp.where(gate == mx, blk, float(nb)), axis=0, keepdims=True)
            pick = blk == idx
            sel = jnp.where(pick & (blk < own), 1.0, sel)
            gate = jnp.where(pick, -jnp.inf, gate)
        off = jnp.where(sel > 0.0, -slope * float(tq) * (own - blk), MASKED)
        for r in range(nb):
            off_ref[h, r] = off[r:r + 1, :]
        bias_ref[h] = -slope * rel
        z = lax.dot_general(k_own, qh[h], NT_DIMS, preferred_element_type=F32)
        s = jnp.where(causal, z + bias_ref[h], MASKED)
        m = jnp.max(s, axis=0, keepdims=True)
        p = jnp.exp2(s - m)
        m_ref[h] = m
        l_ref[h] = jnp.sum(p, axis=0, keepdims=True)
        acc_ref[h] = jnp.dot(vt_own, p.astype(BF16), preferred_element_type=F32)

    def blocks(js):
        ks = [k_ref[0, pl.ds(pl.multiple_of(j * tq, tq), tq), :] for j in js]
        vts = [vt_ref[0, :, pl.ds(pl.multiple_of(j * tq, tq), tq)] for j in js]
        for h in range(2):
            us = [lax.dot_general(k2, qh[h], NT_DIMS, preferred_element_type=F32) + bias_ref[h] for k2 in ks]
            offs = [off_ref[h, j] for j in js]
            m_old = m_ref[h]
            m_new = m_old
            for u, off in zip(us, offs):
                m_new = jnp.maximum(m_new, jnp.max(u, axis=0, keepdims=True) + off)
            alpha = jnp.exp2(m_old - m_new)
            l_new = alpha * l_ref[h]
            acc = alpha * acc_ref[h]
            for u, off, vt in zip(us, offs, vts):
                p = jnp.exp2(u + (off - m_new))
                l_new = l_new + jnp.sum(p, axis=0, keepdims=True)
                acc = acc + jnp.dot(vt, p.astype(BF16), preferred_element_type=F32)
            m_ref[h] = m_new
            l_ref[h] = l_new
            acc_ref[h] = acc

    def group(t, carry):
        blocks([jnp.minimum(group_size * t + g, nb - 1) for g in range(group_size)])
        return carry

    lax.fori_loop(0, (i + group_size - 1) // group_size, group, 0)

    row = lax.broadcasted_iota(jnp.int32, (LANES, tq), 0)
    o_t = jnp.where(row < dh, acc_ref[0] / l_ref[0], acc_ref[1] / l_ref[1])
    o_ref[0] = o_t.T.astype(o_ref.dtype)


def _moba_attention(q, k, vt, kmeans, slopes, group_size=4):
    b, s, d = q.shape
    dh = d // N_HEADS
    tq = MOBA_BLOCK
    assert 2 * dh == LANES and s % tq == 0
    nb = s // tq
    npair = d // LANES
    return pl.pallas_call(
        functools.partial(_moba_body, tq=tq, dh=dh, nb=nb, topk=min(MOBA_TOPK, nb), group_size=group_size),
        grid=(b, npair, nb),
        in_specs=[pl.BlockSpec(memory_space=pltpu.SMEM),
                  pl.BlockSpec((1, tq, LANES), lambda bi, hp, i: (bi, i, hp)),
                  pl.BlockSpec((1, s, LANES), lambda bi, hp, i: (bi, 0, hp)),
                  pl.BlockSpec((1, LANES, s), lambda bi, hp, i: (bi, hp, 0)),
                  pl.BlockSpec((1, nb, LANES), lambda bi, hp, i: (bi, 0, hp))],
        out_specs=pl.BlockSpec((1, tq, LANES), lambda bi, hp, i: (bi, i, hp)),
        out_shape=jax.ShapeDtypeStruct((b, s, d), BF16),
        scratch_shapes=[pltpu.VMEM((2, LANES, tq), F32),
                        pltpu.VMEM((2, 1, tq), F32),
                        pltpu.VMEM((2, 1, tq), F32),
                        pltpu.VMEM((2, tq, tq), F32),
                        pltpu.VMEM((2, nb, 1, tq), F32)],
        compiler_params=_params("parallel", "parallel", "arbitrary"),
        name="moba_attention",
    )(slopes, q, k, vt, kmeans)


def _proj_res_body(o_ref, w_ref, x_ref, gt_ref, out_ref):
    y = jnp.dot(o_ref[0], w_ref[...], preferred_element_type=F32)
    out_ref[0] = x_ref[0] + (1.0 + gt_ref[0]) * y


def _proj_residual(o, w, x, mod, gate_idx, tm=512):
    b, s, d = x.shape
    return pl.pallas_call(
        _proj_res_body,
        grid=(b, s // tm),
        in_specs=[pl.BlockSpec((1, tm, d), lambda bi, i: (bi, i, 0)),
                  pl.BlockSpec((d, d), lambda bi, i: (0, 0)),
                  pl.BlockSpec((1, tm, d), lambda bi, i: (bi, i, 0)),
                  pl.BlockSpec((1, 1, d), lambda bi, i: (bi, 0, gate_idx))],
        out_specs=pl.BlockSpec((1, tm, d), lambda bi, i: (bi, i, 0)),
        out_shape=jax.ShapeDtypeStruct((b, s, d), F32),
        compiler_params=_params("parallel", "parallel"),
        name="proj_residual",
    )(o, w, x, mod)


def _route(logits):
    tm = logits.shape[0]
    lane = lax.broadcasted_iota(jnp.int32, (tm, LANES), 1).astype(F32)
    big = float(LANES)
    is_grp = (lane >= float(N_EXPERTS)) & (lane < float(N_EXPERTS + N_GROUPS))
    lg = jnp.where(is_grp, logits, -jnp.inf)
    mg = jnp.max(lg, axis=-1, keepdims=True)
    p_g = 1.0 / jnp.sum(jnp.exp(lg - mg), axis=-1, keepdims=True)
    g_sel = jnp.min(jnp.where(lg == mg, lane, big), axis=-1, keepdims=True) - float(N_EXPERTS)
    lo = g_sel * float(EXPERTS_PER_GROUP)
    in_grp = (lane >= lo) & (lane < lo + float(EXPERTS_PER_GROUP))
    le = jnp.where(in_grp, logits, -jnp.inf)
    m1 = jnp.max(le, axis=-1, keepdims=True)
    i1 = jnp.min(jnp.where(le == m1, lane, big), axis=-1, keepdims=True)
    le2 = jnp.where(lane == i1, -jnp.inf, le)
    m2 = jnp.max(le2, axis=-1, keepdims=True)
    i2 = jnp.min(jnp.where(le2 == m2, lane, big), axis=-1, keepdims=True)
    e2 = jnp.exp(m2 - m1)
    den = 1.0 + e2
    return jnp.where(lane == i1, p_g / den, 0.0) + jnp.where(lane == i2, p_g * e2 / den, 0.0)


def _moe_body(x_ref, g_ref, sh_ref, sc_ref, gt_ref, wr_ref, br_ref, wg_ref, wu_ref, wd_ref, gf_ref,
              out_ref, h_ref, comb_ref, acc_ref, *, final_norm):
    gi = pl.program_id(1)

    @pl.when(gi == 0)
    def _():
        h = _norm_mod(x_ref[...], g_ref[...], sh_ref[0], sc_ref[0])
        h_ref[...] = h.astype(BF16)
        logits = jnp.dot(h, wr_ref[...], preferred_element_type=F32,
                         precision=lax.Precision.HIGHEST) + br_ref[...]
        comb_ref[...] = _route(logits)
        acc_ref[...] = jnp.zeros_like(acc_ref)

    h = h_ref[...]
    comb = comb_ref[...]
    lane = lax.broadcasted_iota(jnp.int32, comb.shape, 1)
    for e in range(EXPERTS_PER_GROUP):
        ce = jnp.sum(jnp.where(lane == gi * EXPERTS_PER_GROUP + e, comb, 0.0), axis=-1, keepdims=True)
        hg = jnp.dot(h, wg_ref[e], preferred_element_type=F32)
        hu = jnp.dot(h, wu_ref[e], preferred_element_type=F32)
        hid = hg * (1.0 / (1.0 + jnp.exp(-hg))) * hu * ce
        acc_ref[...] += jnp.dot(hid.astype(BF16), wd_ref[e], preferred_element_type=F32)

    @pl.when(gi == N_GROUPS - 1)
    def _():
        y = x_ref[...] + (1.0 + gt_ref[0]) * acc_ref[...]
        if final_norm:
            ms = jnp.mean(y * y, axis=-1, keepdims=True)
            y = y * lax.rsqrt(ms + RMS_EPS) * gf_ref[...]
        out_ref[...] = y


def _moe(x, g, mod, shift_idx, scale_idx, gate_idx, w_router, b_router, w_gate, w_up, w_down, g_final,
         final_norm, tm=512):
    b, s, d = x.shape
    ff = w_gate.shape[2]
    t = b * s
    per_batch = s // tm
    epg = EXPERTS_PER_GROUP
    out = pl.pallas_call(
        functools.partial(_moe_body, final_norm=final_norm),
        grid=(t // tm, N_GROUPS),
        in_specs=[pl.BlockSpec((tm, d), lambda i, gi: (i, 0)),
                  pl.BlockSpec((1, d), lambda i, gi: (0, 0)),
                  pl.BlockSpec((1, 1, d), lambda i, gi: (i // per_batch, 0, shift_idx)),
                  pl.BlockSpec((1, 1, d), lambda i, gi: (i // per_batch, 0, scale_idx)),
                  pl.BlockSpec((1, 1, d), lambda i, gi: (i // per_batch, 0, gate_idx)),
                  pl.BlockSpec((d, LANES), lambda i, gi: (0, 0)),
                  pl.BlockSpec((1, LANES), lambda i, gi: (0, 0)),
                  pl.BlockSpec((epg, d, ff), lambda i, gi: (gi, 0, 0)),
                  pl.BlockSpec((epg, d, ff), lambda i, gi: (gi, 0, 0)),
                  pl.BlockSpec((epg, ff, d), lambda i, gi: (gi, 0, 0)),
                  pl.BlockSpec((1, d), lambda i, gi: (0, 0))],
        out_specs=pl.BlockSpec((tm, d), lambda i, gi: (i, 0)),
        out_shape=jax.ShapeDtypeStruct((t, d), F32),
        scratch_shapes=[pltpu.VMEM((tm, d), BF16),
                        pltpu.VMEM((tm, LANES), F32),
                        pltpu.VMEM((tm, d), F32)],
        compiler_params=_params("parallel", "arbitrary"),
        name="moe_final" if final_norm else "moe",
    )(x.reshape(t, d), g.reshape(1, d), mod, mod, mod, w_router, b_router, w_gate, w_up, w_down,
      g_final.reshape(1, d))
    return out.reshape(b, s, d)


def _router_params(w_rg, b_rg, w_re, b_re):
    d = w_rg.shape[0]
    pad = LANES - N_EXPERTS - N_GROUPS
    w = jnp.concatenate([w_re, w_rg, jnp.zeros((d, pad), F32)], axis=1)
    bias = jnp.concatenate([b_re, b_rg, jnp.zeros((pad,), F32)]).reshape(1, LANES)
    return w, bias


def kernel(x, c, w_ada, b_ada, g_attn, g_ffn, w_qkv_a, w_o_a, w_ada_kv, b_ada_kv, g_kv, w_kv, w_q_b, w_o_b,
           w_rg, b_rg, w_re, b_re, w_gate, w_up, w_down, g_final):
    b, s, d = x.shape
    depth = w_ada.shape[0]
    n_a = depth // 2
    q_scale = LOG2E * (d // N_HEADS) ** -0.5
    slopes = LOG2E * jnp.exp2(-ALIBI_MAX_BIAS * jnp.arange(1, N_HEADS + 1, dtype=F32) / N_HEADS)
    mods = _ada(c, w_ada, b_ada).reshape(depth, b, 1, 6 * d)
    mod_kv = _ada(c, w_ada_kv[None], b_ada_kv[None]).reshape(b, 1, 2 * d)
    k = vt = kmeans = None
    for l in range(depth):
        mod = mods[l]
        if l < n_a:
            w_qkv = jnp.concatenate([w_qkv_a[l][:, :d] * q_scale, w_qkv_a[l][:, d:]], axis=1).astype(BF16)
            qk, v_t = _norm_matmul(x, g_attn[l], mod, 0, 1, w_qkv, with_t=True)
            o = _sb_attention(qk, v_t, d)
            w_o = w_o_a[l]
        else:
            j = l - n_a
            q, = _norm_matmul(x, g_attn[l], mod, 0, 1, (w_q_b[j] * q_scale).astype(BF16))
            o = _moba_attention(q, k, vt, kmeans, slopes)
            w_o = w_o_b[j]
        x = _proj_residual(o, w_o.astype(BF16), x, mod, 2)
        w_router, b_router = _router_params(w_rg[l], b_rg[l], w_re[l], b_re[l])
        x = _moe(x, g_ffn[l], mod, 3, 4, 5, w_router, b_router, w_gate[l].astype(BF16), w_up[l].astype(BF16),
                 w_down[l].astype(BF16), g_final, final_norm=(l == depth - 1))
        if l == n_a - 1:
            k, vt, kmeans = _norm_matmul(x, g_kv, mod_kv, 0, 1, w_kv.astype(BF16), with_t=True, with_means=True)
            kmeans = kmeans.reshape(b, s // MOBA_BLOCK, d)
    return x
```

```python
import functools
import math

import jax
import jax.numpy as jnp
from jax import lax
from jax.experimental import pallas as pl
from jax.experimental.pallas import tpu as pltpu

N_HEADS = 16
MOBA_BLOCK = 256
MOBA_TOPK = 3
N_GROUPS = 4
EXPERTS_PER_GROUP = 4
N_EXPERTS = N_GROUPS * EXPERTS_PER_GROUP
RMS_EPS = 1e-6
ALIBI_MAX_BIAS = 8.0

LANES = 128
VMEM_LIMIT = 56 * 1024 * 1024
MASKED = -1e30
LOG2E = math.log2(math.e)
DEAD_LOG2 = -160.0

F32 = jnp.float32
BF16 = jnp.bfloat16
NT_DIMS = (((1,), (1,)), ((), ()))


def _params(*sem):
    return pltpu.CompilerParams(dimension_semantics=sem, vmem_limit_bytes=VMEM_LIMIT)


def _ada_body(c_ref, w_ref, b_ref, o_ref):
    o_ref[0] = jnp.dot(c_ref[...], w_ref[0], preferred_element_type=F32,
                       precision=lax.Precision.HIGHEST) + b_ref[0]


def _ada(c, w, b, tn=512):
    nl, d, n = w.shape
    bsz = c.shape[0]
    return pl.pallas_call(
        _ada_body,
        grid=(nl, n // tn),
        in_specs=[pl.BlockSpec((bsz, d), lambda l, j: (0, 0)),
                  pl.BlockSpec((1, d, tn), lambda l, j: (l, 0, j)),
                  pl.BlockSpec((1, 1, tn), lambda l, j: (l, 0, j))],
        out_specs=pl.BlockSpec((1, bsz, tn), lambda l, j: (l, 0, j)),
        out_shape=jax.ShapeDtypeStruct((nl, bsz, n), F32),
        compiler_params=_params("parallel", "parallel"),
        name="ada_mod",
    )(c, w, b.reshape(nl, 1, n))


def _norm_mod(x, g, shift, scale):
    ms = jnp.mean(x * x, axis=-1, keepdims=True)
    y = x * lax.rsqrt(ms + RMS_EPS) * g
    return y * (1.0 + scale) + shift


def _norm_mm_body(x_ref, g_ref, sh_ref, sc_ref, w_ref, o_ref, *rest, n_plain, d_model, with_t, with_means):
    h = _norm_mod(x_ref[0], g_ref[...], sh_ref[0], sc_ref[0])
    y = jnp.dot(h.astype(BF16), w_ref[...], preferred_element_type=F32)
    o_ref[0] = y[:, :n_plain].astype(o_ref.dtype)
    if with_t:
        rest[0][0] = y[:, n_plain:].T.astype(BF16)
    if with_means:
        km_ref = rest[-1]
        for r in range(y.shape[0] // MOBA_BLOCK):
            blk = y[r * MOBA_BLOCK:(r + 1) * MOBA_BLOCK, :d_model]
            km_ref[0, r] = jnp.mean(blk, axis=0, keepdims=True)


def _norm_matmul(x, g, mod, shift_idx, scale_idx, w, with_t=False, with_means=False, tm=512):
    b, s, d = x.shape
    n = w.shape[1]
    n_plain = n - d if with_t else n
    out_shape = [jax.ShapeDtypeStruct((b, s, n_plain), BF16)]
    out_specs = [pl.BlockSpec((1, tm, n_plain), lambda bi, i: (bi, i, 0))]
    if with_t:
        out_shape.append(jax.ShapeDtypeStruct((b, d, s), BF16))
        out_specs.append(pl.BlockSpec((1, d, tm), lambda bi, i: (bi, 0, i)))
    if with_means:
        out_shape.append(jax.ShapeDtypeStruct((b, s // MOBA_BLOCK, 1, d), F32))
        out_specs.append(pl.BlockSpec((1, tm // MOBA_BLOCK, 1, d), lambda bi, i: (bi, i, 0, 0)))
    return pl.pallas_call(
        functools.partial(_norm_mm_body, n_plain=n_plain, d_model=d, with_t=with_t, with_means=with_means),
        grid=(b, s // tm),
        in_specs=[pl.BlockSpec((1, tm, d), lambda bi, i: (bi, i, 0)),
                  pl.BlockSpec((1, d), lambda bi, i: (0, 0)),
                  pl.BlockSpec((1, 1, d), lambda bi, i: (bi, 0, shift_idx)),
                  pl.BlockSpec((1, 1, d), lambda bi, i: (bi, 0, scale_idx)),
                  pl.BlockSpec((d, n), lambda bi, i: (0, 0))],
        out_specs=out_specs,
        out_shape=out_shape,
        compiler_params=_params("parallel", "parallel"),
        name="norm_matmul" + ("_t" if with_t else "") + ("_means" if with_means else ""),
    )(x, g.reshape(1, d), mod, mod, w)


def _head_pair(q2, dh):
    lane = lax.broadcasted_iota(jnp.int32, q2.shape, 1)
    zero = jnp.zeros_like(q2)
    return jnp.where(lane < dh, q2, zero), jnp.where(lane >= dh, q2, zero)


def _sb_body(q_ref, k_ref, vt_ref, o_ref, acc_ref, *, tq, dh, group_size):
    i = pl.program_id(2)
    qh = _head_pair(q_ref[0], dh)
    key = lax.broadcasted_iota(jnp.int32, (tq, tq), 0)
    qry = lax.broadcasted_iota(jnp.int32, (tq, tq), 1)
    past = key < qry
    neg_upper = jnp.where(qry > key, -1.0, 0.0).astype(BF16)
    sign_bit = jnp.uint32(0x80000000)

    acc_ref[...] = jnp.zeros_like(acc_ref)

    def blocks(js, carries, diag=False):
        ks = [k_ref[0, pl.ds(pl.multiple_of(j * tq, tq), tq), :] for j in js]
        vts = [vt_ref[0, :, pl.ds(pl.multiple_of(j * tq, tq), tq)] for j in js]
        chains = [(b, h) for b in range(len(js)) for h in range(2)]
        carry = list(carries)
        acc = [acc_ref[0], acc_ref[1]]
        z, lbeta, sp, sp0, excl, w = {}, {}, {}, {}, {}, {}

        def stage(c, s):
            b, h = c
            if s == 0:
                z[c] = lax.dot_general(ks[b], qh[h], NT_DIMS, preferred_element_type=F32)
            elif s == 1:
                zc = z.pop(c)
                neg_abs = lax.bitcast_convert_type(lax.bitcast_convert_type(zc, jnp.uint32) | sign_bit, F32)
                v = jnp.maximum(zc, 0.0) + jnp.log2(1.0 + jnp.exp2(neg_abs))
                lbeta[c] = zc - v
                if diag:
                    v = jnp.where(past, v, 0.0)
                sp0[c] = v[0:1, :]
                sp[c] = v.astype(BF16)
            elif s == 2:
                excl[c] = jnp.dot(neg_upper, sp.pop(c), preferred_element_type=F32)
            elif s == 3:
                v = jnp.exp2(lbeta.pop(c) + excl[c] + carry[h])
                if diag:
                    v = jnp.where(past, v, 0.0)
                w[c] = v.astype(BF16)
                carry[h] = carry[h] + excl.pop(c)[0:1, :] - sp0.pop(c)
            else:
                acc[h] = acc[h] + jnp.dot(vts[b], w.pop(c), preferred_element_type=F32)

        n_stages = 5
        for t in range(len(chains) + n_stages - 1):
            for ci, c in enumerate(chains):
                if 0 <= t - ci < n_stages:
                    stage(c, t - ci)
        acc_ref[0] = acc[0]
        acc_ref[1] = acc[1]
        return tuple(carry)

    def alive(carries):
        return (jnp.max(jnp.maximum(carries[0], carries[1])) > DEAD_LOG2).astype(jnp.int32)

    def walk(n, tiles_of, state):
        def body(st):
            carries = blocks(tiles_of(st[0]), (st[2], st[3]))
            return st[0] + 1, alive(carries), carries[0], carries[1]
        st = lax.while_loop(lambda st: (st[0] < n) & (st[1] > 0), body, (jnp.int32(0),) + state)
        return st[1:]

    zeros = jnp.zeros((1, tq), F32)
    carries = blocks([i], (zeros, zeros), diag=True)
    n_single = jnp.where(i > 0, i - group_size * ((i - 1) // group_size), 0)
    state = walk(n_single, lambda t: [i - 1 - t], (alive(carries),) + carries)
    top = i - 1 - n_single
    walk((i - n_single) // group_size,
         lambda t: [top - group_size * t - g for g in range(group_size)], state)
    row = lax.broadcasted_iota(jnp.int32, (LANES, tq), 0)
    o_ref[0] = jnp.where(row < dh, acc_ref[0], acc_ref[1]).T.astype(o_ref.dtype)


def _sb_attention(qk, vt, d, tq=256, group_size=4):
    b, s, _ = qk.shape
    dh = d // N_HEADS
    assert 2 * dh == LANES and s % tq == 0
    npair = d // LANES
    return pl.pallas_call(
        functools.partial(_sb_body, tq=tq, dh=dh, group_size=group_size),
        grid=(b, npair, s // tq),
        in_specs=[pl.BlockSpec((1, tq, LANES), lambda bi, hp, i: (bi, i, hp)),
                  pl.BlockSpec((1, s, LANES), lambda bi, hp, i: (bi, 0, npair + hp)),
                  pl.BlockSpec((1, LANES, s), lambda bi, hp, i: (bi, hp, 0))],
        out_specs=pl.BlockSpec((1, tq, LANES), lambda bi, hp, i: (bi, i, hp)),
        out_shape=jax.ShapeDtypeStruct((b, s, d), BF16),
        scratch_shapes=[pltpu.VMEM((2, LANES, tq), F32)],
        compiler_params=_params("parallel", "parallel", "arbitrary"),
        name="sb_attention",
    )(qk, qk, vt)


def _moba_body(slopes_ref, q_ref, k_ref, vt_ref, km_ref, o_ref,
               acc_ref, m_ref, l_ref, bias_ref, off_ref, *, tq, dh, nb, topk, group_size):
    hp = pl.program_id(1)
    i = pl.program_id(2)
    qh = _head_pair(q_ref[0], dh)
    km = km_ref[0]
    key = lax.broadcasted_iota(jnp.int32, (tq, tq), 0)
    qry = lax.broadcasted_iota(jnp.int32, (tq, tq), 1)
    rel = (qry - key).astype(F32)
    causal = qry >= key
    blk = lax.broadcasted_iota(jnp.int32, (nb, tq), 0).astype(F32)
    own = i.astype(F32)
    start_own = pl.multiple_of(i * tq, tq)
    k_own = k_ref[0, pl.ds(start_own, tq), :]
    vt_own = vt_ref[0, :, pl.ds(start_own, tq)]

    for h in range(2):
        slope = slopes_ref[2 * hp + h]
        gate = lax.dot_general(km, qh[h].astype(F32), NT_DIMS, preferred_element_type=F32,
                               precision=lax.Precision.HIGHEST)
        gate = jnp.where(blk < own, gate, -jnp.inf)
        sel = jnp.zeros((nb, tq), F32)
        for _ in range(topk):
            mx = jnp.max(gate, axis=0, keepdims=True)
            idx = jnp.min(jnp.where(gate == mx, blk, float(nb)), axis=0, keepdims=True)
            pick = blk == idx
            sel = jnp.where(pick & (blk < own), 1.0, sel)
            gate = jnp.where(pick, -jnp.inf, gate)
        off = jnp.where(sel > 0.0, -slope * float(tq) * (own - blk), MASKED)
        for r in range(nb):
            off_ref[h, r] = off[r:r + 1, :]
        bias_ref[h] = -slope * rel
        z = lax.dot_general(k_own, qh[h], NT_DIMS, preferred_element_type=F32)
        s = jnp.where(causal, z + bias_ref[h], MASKED)
        m = jnp.max(s, axis=0, keepdims=True)
        p = jnp.exp2(s - m)
        m_ref[h] = m
        l_ref[h] = jnp.sum(p, axis=0, keepdims=True)
        acc_ref[h] = jnp.dot(vt_own, p.astype(BF16), preferred_element_type=F32)

    def blocks(js):
        ks = [k_ref[0, pl.ds(pl.multiple_of(j * tq, tq), tq), :] for j in js]
        vts = [vt_ref[0, :, pl.ds(pl.multiple_of(j * tq, tq), tq)] for j in js]
        for h in range(2):
            us = [lax.dot_general(k2, qh[h], NT_DIMS, preferred_element_type=F32) + bias_ref[h] for k2 in ks]
            offs = [off_ref[h, j] for j in js]
            m_old = m_ref[h]
            m_new = m_old
            for u, off in zip(us, offs):
                m_new = jnp.maximum(m_new, jnp.max(u, axis=0, keepdims=True) + off)
            alpha = jnp.exp2(m_old - m_new)
            l_new = alpha * l_ref[h]
            acc = alpha * acc_ref[h]
            for u, off, vt in zip(us, offs, vts):
                p = jnp.exp2(u + (off - m_new))
                l_new = l_new + jnp.sum(p, axis=0, keepdims=True)
                acc = acc + jnp.dot(vt, p.astype(BF16), preferred_element_type=F32)
            m_ref[h] = m_new
            l_ref[h] = l_new
            acc_ref[h] = acc

    def group(t, carry):
        blocks([jnp.minimum(group_size * t + g, nb - 1) for g in range(group_size)])
        return carry

    lax.fori_loop(0, (i + group_size - 1) // group_size, group, 0)

    row = lax.broadcasted_iota(jnp.int32, (LANES, tq), 0)
    o_t = jnp.where(row < dh, acc_ref[0] / l_ref[0], acc_ref[1] / l_ref[1])
    o_ref[0] = o_t.T.astype(o_ref.dtype)


def _moba_attention(q, k, vt, kmeans, slopes, group_size=4):
    b, s, d = q.shape
    dh = d // N_HEADS
    tq = MOBA_BLOCK
    assert 2 * dh == LANES and s % tq == 0
    nb = s // tq
    npair = d // LANES
    return pl.pallas_call(
        functools.partial(_moba_body, tq=tq, dh=dh, nb=nb, topk=min(MOBA_TOPK, nb), group_size=group_size),
        grid=(b, npair, nb),
        in_specs=[pl.BlockSpec(memory_space=pltpu.SMEM),
                  pl.BlockSpec((1, tq, LANES), lambda bi, hp, i: (bi, i, hp)),
                  pl.BlockSpec((1, s, LANES), lambda bi, hp, i: (bi, 0, hp)),
                  pl.BlockSpec((1, LANES, s), lambda bi, hp, i: (bi, hp, 0)),
                  pl.BlockSpec((1, nb, LANES), lambda bi, hp, i: (bi, 0, hp))],
        out_specs=pl.BlockSpec((1, tq, LANES), lambda bi, hp, i: (bi, i, hp)),
        out_shape=jax.ShapeDtypeStruct((b, s, d), BF16),
        scratch_shapes=[pltpu.VMEM((2, LANES, tq), F32),
                        pltpu.VMEM((2, 1, tq), F32),
                        pltpu.VMEM((2, 1, tq), F32),
                        pltpu.VMEM((2, tq, tq), F32),
                        pltpu.VMEM((2, nb, 1, tq), F32)],
        compiler_params=_params("parallel", "parallel", "arbitrary"),
        name="moba_attention",
    )(slopes, q, k, vt, kmeans)


def _proj_res_body(o_ref, w_ref, x_ref, gt_ref, out_ref):
    y = jnp.dot(o_ref[0], w_ref[...], preferred_element_type=F32)
    out_ref[0] = x_ref[0] + (1.0 + gt_ref[0]) * y


def _proj_residual(o, w, x, mod, gate_idx, tm=512):
    b, s, d = x.shape
    return pl.pallas_call(
        _proj_res_body,
        grid=(b, s // tm),
        in_specs=[pl.BlockSpec((1, tm, d), lambda bi, i: (bi, i, 0)),
                  pl.BlockSpec((d, d), lambda bi, i: (0, 0)),
                  pl.BlockSpec((1, tm, d), lambda bi, i: (bi, i, 0)),
                  pl.BlockSpec((1, 1, d), lambda bi, i: (bi, 0, gate_idx))],
        out_specs=pl.BlockSpec((1, tm, d), lambda bi, i: (bi, i, 0)),
        out_shape=jax.ShapeDtypeStruct((b, s, d), F32),
        compiler_params=_params("parallel", "parallel"),
        name="proj_residual",
    )(o, w, x, mod)


def _route(logits):
    tm = logits.shape[0]
    lane = lax.broadcasted_iota(jnp.int32, (tm, LANES), 1).astype(F32)
    big = float(LANES)
    is_grp = (lane >= float(N_EXPERTS)) & (lane < float(N_EXPERTS + N_GROUPS))
    lg = jnp.where(is_grp, logits, -jnp.inf)
    mg = jnp.max(lg, axis=-1, keepdims=True)
    p_g = 1.0 / jnp.sum(jnp.exp(lg - mg), axis=-1, keepdims=True)
    g_sel = jnp.min(jnp.where(lg == mg, lane, big), axis=-1, keepdims=True) - float(N_EXPERTS)
    lo = g_sel * float(EXPERTS_PER_GROUP)
    in_grp = (lane >= lo) & (lane < lo + float(EXPERTS_PER_GROUP))
    le = jnp.where(in_grp, logits, -jnp.inf)
    m1 = jnp.max(le, axis=-1, keepdims=True)
    i1 = jnp.min(jnp.where(le == m1, lane, big), axis=-1, keepdims=True)
    le2 = jnp.where(lane == i1, -jnp.inf, le)
    m2 = jnp.max(le2, axis=-1, keepdims=True)
    i2 = jnp.min(jnp.where(le2 == m2, lane, big), axis=-1, keepdims=True)
    e2 = jnp.exp(m2 - m1)
    den = 1.0 + e2
    return jnp.where(lane == i1, p_g / den, 0.0) + jnp.where(lane == i2, p_g * e2 / den, 0.0)


def _moe_body(x_ref, g_ref, sh_ref, sc_ref, gt_ref, wr_ref, br_ref, wg_ref, wu_ref, wd_ref, gf_ref,
              out_ref, h_ref, comb_ref, acc_ref, *, final_norm):
    gi = pl.program_id(1)

    @pl.when(gi == 0)
    def _():
        h = _norm_mod(x_ref[...], g_ref[...], sh_ref[0], sc_ref[0])
        h_ref[...] = h.astype(BF16)
        logits = jnp.dot(h, wr_ref[...], preferred_element_type=F32,
                         precision=lax.Precision.HIGHEST) + br_ref[...]
        comb_ref[...] = _route(logits)
        acc_ref[...] = jnp.zeros_like(acc_ref)

    h = h_ref[...]
    comb = comb_ref[...]
    lane = lax.broadcasted_iota(jnp.int32, comb.shape, 1)
    for e in range(EXPERTS_PER_GROUP):
        ce = jnp.sum(jnp.where(lane == gi * EXPERTS_PER_GROUP + e, comb, 0.0), axis=-1, keepdims=True)
        hg = jnp.dot(h, wg_ref[e], preferred_element_type=F32)
        hu = jnp.dot(h, wu_ref[e], preferred_element_type=F32)
        hid = hg * (1.0 / (1.0 + jnp.exp(-hg))) * hu * ce
        acc_ref[...] += jnp.dot(hid.astype(BF16), wd_ref[e], preferred_element_type=F32)

    @pl.when(gi == N_GROUPS - 1)
    def _():
        y = x_ref[...] + (1.0 + gt_ref[0]) * acc_ref[...]
        if final_norm:
            ms = jnp.mean(y * y, axis=-1, keepdims=True)
            y = y * lax.rsqrt(ms + RMS_EPS) * gf_ref[...]
        out_ref[...] = y


def _moe(x, g, mod, shift_idx, scale_idx, gate_idx, w_router, b_router, w_gate, w_up, w_down, g_final,
         final_norm, tm=512):
    b, s, d = x.shape
    ff = w_gate.shape[2]
    t = b * s
    per_batch = s // tm
    epg = EXPERTS_PER_GROUP
    out = pl.pallas_call(
        functools.partial(_moe_body, final_norm=final_norm),
        grid=(t // tm, N_GROUPS),
        in_specs=[pl.BlockSpec((tm, d), lambda i, gi: (i, 0)),
                  pl.BlockSpec((1, d), lambda i, gi: (0, 0)),
                  pl.BlockSpec((1, 1, d), lambda i, gi: (i // per_batch, 0, shift_idx)),
                  pl.BlockSpec((1, 1, d), lambda i, gi: (i // per_batch, 0, scale_idx)),
                  pl.BlockSpec((1, 1, d), lambda i, gi: (i // per_batch, 0, gate_idx)),
                  pl.BlockSpec((d, LANES), lambda i, gi: (0, 0)),
                  pl.BlockSpec((1, LANES), lambda i, gi: (0, 0)),
                  pl.BlockSpec((epg, d, ff), lambda i, gi: (gi, 0, 0)),
                  pl.BlockSpec((epg, d, ff), lambda i, gi: (gi, 0, 0)),
                  pl.BlockSpec((epg, ff, d), lambda i, gi: (gi, 0, 0)),
                  pl.BlockSpec((1, d), lambda i, gi: (0, 0))],
        out_specs=pl.BlockSpec((tm, d), lambda i, gi: (i, 0)),
        out_shape=jax.ShapeDtypeStruct((t, d), F32),
        scratch_shapes=[pltpu.VMEM((tm, d), BF16),
                        pltpu.VMEM((tm, LANES), F32),
                        pltpu.VMEM((tm, d), F32)],
        compiler_params=_params("parallel", "arbitrary"),
        name="moe_final" if final_norm else "moe",
    )(x.reshape(t, d), g.reshape(1, d), mod, mod, mod, w_router, b_router, w_gate, w_up, w_down,
      g_final.reshape(1, d))
    return out.reshape(b, s, d)


def _router_params(w_rg, b_rg, w_re, b_re):
    d = w_rg.shape[0]
    pad = LANES - N_EXPERTS - N_GROUPS
    w = jnp.concatenate([w_re, w_rg, jnp.zeros((d, pad), F32)], axis=1)
    bias = jnp.concatenate([b_re, b_rg, jnp.zeros((pad,), F32)]).reshape(1, LANES)
    return w, bias


def kernel(x, c, w_ada, b_ada, g_attn, g_ffn, w_qkv_a, w_o_a, w_ada_kv, b_ada_kv, g_kv, w_kv, w_q_b, w_o_b,
           w_rg, b_rg, w_re, b_re, w_gate, w_up, w_down, g_final):
    b, s, d = x.shape
    depth = w_ada.shape[0]
    n_a = depth // 2
    q_scale = LOG2E * (d // N_HEADS) ** -0.5
    slopes = LOG2E * jnp.exp2(-ALIBI_MAX_BIAS * jnp.arange(1, N_HEADS + 1, dtype=F32) / N_HEADS)
    mods = _ada(c, w_ada, b_ada).reshape(depth, b, 1, 6 * d)
    mod_kv = _ada(c, w_ada_kv[None], b_ada_kv[None]).reshape(b, 1, 2 * d)
    k = vt = kmeans = None
    for l in range(depth):
        mod = mods[l]
        if l < n_a:
            w_qkv = jnp.concatenate([w_qkv_a[l][:, :d] * q_scale, w_qkv_a[l][:, d:]], axis=1).astype(BF16)
            qk, v_t = _norm_matmul(x, g_attn[l], mod, 0, 1, w_qkv, with_t=True)
            o = _sb_attention(qk, v_t, d)
            w_o = w_o_a[l]
        else:
            j = l - n_a
            q, = _norm_matmul(x, g_attn[l], mod, 0, 1, (w_q_b[j] * q_scale).astype(BF16))
            o = _moba_attention(q, k, vt, kmeans, slopes)
            w_o = w_o_b[j]
        x = _proj_residual(o, w_o.astype(BF16), x, mod, 2)
        w_router, b_router = _router_params(w_rg[l], b_rg[l], w_re[l], b_re[l])
        x = _moe(x, g_ffn[l], mod, 3, 4, 5, w_router, b_router, w_gate[l].astype(BF16), w_up[l].astype(BF16),
                 w_down[l].astype(BF16), g_final, final_norm=(l == depth - 1))
        if l == n_a - 1:
            k, vt, kmeans = _norm_matmul(x, g_kv, mod_kv, 0, 1, w_kv.astype(BF16), with_t=True, with_means=True)
            kmeans = kmeans.reshape(b, s // MOBA_BLOCK, d)
    return x
```

```python
import functools
import math

import jax
import jax.numpy as jnp
from jax import lax
from jax.experimental import pallas as pl
from jax.experimental.pallas import tpu as pltpu

N_HEADS = 16
MOBA_BLOCK = 256
MOBA_TOPK = 3
N_GROUPS = 4
EXPERTS_PER_GROUP = 4
N_EXPERTS = N_GROUPS * EXPERTS_PER_GROUP
RMS_EPS = 1e-6
ALIBI_MAX_BIAS = 8.0

LANES = 128
VMEM_LIMIT = 56 * 1024 * 1024
MASKED = -1e30
LOG2E = math.log2(math.e)
NORM_MARGIN = 1.02
DEAD_LOG2 = -160.0

F32 = jnp.float32
BF16 = jnp.bfloat16
NT_DIMS = (((1,), (1,)), ((), ()))


def _params(*sem):
    return pltpu.CompilerParams(dimension_semantics=sem, vmem_limit_bytes=VMEM_LIMIT)


def _ada_body(c_ref, w_ref, b_ref, o_ref):
    o_ref[0] = jnp.dot(c_ref[...], w_ref[0], preferred_element_type=F32,
                       precision=lax.Precision.HIGHEST) + b_ref[0]


def _ada(c, w, b, tn=512):
    nl, d, n = w.shape
    bsz = c.shape[0]
    return pl.pallas_call(
        _ada_body,
        grid=(nl, n // tn),
        in_specs=[pl.BlockSpec((bsz, d), lambda l, j: (0, 0)),
                  pl.BlockSpec((1, d, tn), lambda l, j: (l, 0, j)),
                  pl.BlockSpec((1, 1, tn), lambda l, j: (l, 0, j))],
        out_specs=pl.BlockSpec((1, bsz, tn), lambda l, j: (l, 0, j)),
        out_shape=jax.ShapeDtypeStruct((nl, bsz, n), F32),
        compiler_params=_params("parallel", "parallel"),
        name="ada_mod",
    )(c, w, b.reshape(nl, 1, n))


def _norm_mod(x, g, shift, scale):
    ms = jnp.mean(x * x, axis=-1, keepdims=True)
    y = x * lax.rsqrt(ms + RMS_EPS) * g
    return y * (1.0 + scale) + shift


def _norm_mm_body(x_ref, g_ref, sh_ref, sc_ref, w_ref, o_ref, *rest, n_plain, d_model, with_t, with_means):
    h = _norm_mod(x_ref[0], g_ref[...], sh_ref[0], sc_ref[0])
    y = jnp.dot(h.astype(BF16), w_ref[...], preferred_element_type=F32)
    o_ref[0] = y[:, :n_plain].astype(o_ref.dtype)
    if with_t:
        rest[0][0] = y[:, n_plain:].T.astype(BF16)
    if with_means:
        km_ref, kn_ref = rest[-2:]
        kb = y[:, :d_model].astype(BF16).astype(F32)
        dcol = lax.broadcasted_iota(jnp.int32, (d_model, LANES), 0)
        head = lax.broadcasted_iota(jnp.int32, (d_model, LANES), 1)
        seg = jnp.where(dcol // (d_model // N_HEADS) == head, 1.0, 0.0).astype(BF16)
        kn2 = jnp.dot((kb * kb).astype(BF16), seg, preferred_element_type=F32)
        for r in range(y.shape[0] // MOBA_BLOCK):
            rows = slice(r * MOBA_BLOCK, (r + 1) * MOBA_BLOCK)
            km_ref[0, r] = jnp.mean(y[rows, :d_model], axis=0, keepdims=True)
            kn_ref[0, r] = jnp.max(kn2[rows], axis=0, keepdims=True)


def _norm_matmul(x, g, mod, shift_idx, scale_idx, w, with_t=False, with_means=False, tm=512):
    b, s, d = x.shape
    n = w.shape[1]
    n_plain = n - d if with_t else n
    out_shape = [jax.ShapeDtypeStruct((b, s, n_plain), BF16)]
    out_specs = [pl.BlockSpec((1, tm, n_plain), lambda bi, i: (bi, i, 0))]
    if with_t:
        out_shape.append(jax.ShapeDtypeStruct((b, d, s), BF16))
        out_specs.append(pl.BlockSpec((1, d, tm), lambda bi, i: (bi, 0, i)))
    if with_means:
        out_shape.append(jax.ShapeDtypeStruct((b, s // MOBA_BLOCK, 1, d), F32))
        out_specs.append(pl.BlockSpec((1, tm // MOBA_BLOCK, 1, d), lambda bi, i: (bi, i, 0, 0)))
        out_shape.append(jax.ShapeDtypeStruct((b, s // MOBA_BLOCK, 1, LANES), F32))
        out_specs.append(pl.BlockSpec((1, tm // MOBA_BLOCK, 1, LANES), lambda bi, i: (bi, i, 0, 0)))
    return pl.pallas_call(
        functools.partial(_norm_mm_body, n_plain=n_plain, d_model=d, with_t=with_t, with_means=with_means),
        grid=(b, s // tm),
        in_specs=[pl.BlockSpec((1, tm, d), lambda bi, i: (bi, i, 0)),
                  pl.BlockSpec((1, d), lambda bi, i: (0, 0)),
                  pl.BlockSpec((1, 1, d), lambda bi, i: (bi, 0, shift_idx)),
                  pl.BlockSpec((1, 1, d), lambda bi, i: (bi, 0, scale_idx)),
                  pl.BlockSpec((d, n), lambda bi, i: (0, 0))],
        out_specs=out_specs,
        out_shape=out_shape,
        compiler_params=_params("parallel", "parallel"),
        name="norm_matmul" + ("_t" if with_t else "") + ("_means" if with_means else ""),
    )(x, g.reshape(1, d), mod, mod, w)


def _head_pair(q2, dh):
    lane = lax.broadcasted_iota(jnp.int32, q2.shape, 1)
    zero = jnp.zeros_like(q2)
    return jnp.where(lane < dh, q2, zero), jnp.where(lane >= dh, q2, zero)


def _sb_body(q_ref, k_ref, vt_ref, o_ref, acc_ref, *, tq, dh, group_size):
    i = pl.program_id(2)
    qh = _head_pair(q_ref[0], dh)
    key = lax.broadcasted_iota(jnp.int32, (tq, tq), 0)
    qry = lax.broadcasted_iota(jnp.int32, (tq, tq), 1)
    past = key < qry
    neg_upper = jnp.where(qry > key, -1.0, 0.0).astype(BF16)
    sign_bit = jnp.uint32(0x80000000)

    acc_ref[...] = jnp.zeros_like(acc_ref)

    def blocks(js, carries, diag=False):
        ks = [k_ref[0, pl.ds(pl.multiple_of(j * tq, tq), tq), :] for j in js]
        vts = [vt_ref[0, :, pl.ds(pl.multiple_of(j * tq, tq), tq)] for j in js]
        chains = [(b, h) for b in range(len(js)) for h in range(2)]
        carry = list(carries)
        acc = [acc_ref[0], acc_ref[1]]
        z, lbeta, sp, sp0, excl, w = {}, {}, {}, {}, {}, {}

        def stage(c, s):
            b, h = c
            if s == 0:
                z[c] = lax.dot_general(ks[b], qh[h], NT_DIMS, preferred_element_type=F32)
            elif s == 1:
                zc = z.pop(c)
                neg_abs = lax.bitcast_convert_type(lax.bitcast_convert_type(zc, jnp.uint32) | sign_bit, F32)
                v = jnp.maximum(zc, 0.0) + jnp.log2(1.0 + jnp.exp2(neg_abs))
                lbeta[c] = zc - v
                if diag:
                    v = jnp.where(past, v, 0.0)
                sp0[c] = v[0:1, :]
                sp[c] = v.astype(BF16)
            elif s == 2:
                excl[c] = jnp.dot(neg_upper, sp.pop(c), preferred_element_type=F32)
            elif s == 3:
                v = jnp.exp2(lbeta.pop(c) + excl[c] + carry[h])
                if diag:
                    v = jnp.where(past, v, 0.0)
                w[c] = v.astype(BF16)
                carry[h] = carry[h] + excl.pop(c)[0:1, :] - sp0.pop(c)
            else:
                acc[h] = acc[h] + jnp.dot(vts[b], w.pop(c), preferred_element_type=F32)

        n_stages = 5
        for t in range(len(chains) + n_stages - 1):
            for ci, c in enumerate(chains):
                if 0 <= t - ci < n_stages:
                    stage(c, t - ci)
        acc_ref[0] = acc[0]
        acc_ref[1] = acc[1]
        return tuple(carry)

    def alive(carries):
        return (jnp.max(jnp.maximum(carries[0], carries[1])) > DEAD_LOG2).astype(jnp.int32)

    def walk(n, tiles_of, state):
        def body(st):
            carries = blocks(tiles_of(st[0]), (st[2], st[3]))
            return st[0] + 1, alive(carries), carries[0], carries[1]
        st = lax.while_loop(lambda st: (st[0] < n) & (st[1] > 0), body, (jnp.int32(0),) + state)
        return st[1:]

    zeros = jnp.zeros((1, tq), F32)
    carries = blocks([i], (zeros, zeros), diag=True)
    n_single = jnp.where(i > 0, i - group_size * ((i - 1) // group_size), 0)
    state = walk(n_single, lambda t: [i - 1 - t], (alive(carries),) + carries)
    top = i - 1 - n_single
    walk((i - n_single) // group_size,
         lambda t: [top - group_size * t - g for g in range(group_size)], state)
    row = lax.broadcasted_iota(jnp.int32, (LANES, tq), 0)
    o_ref[0] = jnp.where(row < dh, acc_ref[0], acc_ref[1]).T.astype(o_ref.dtype)


def _sb_attention(qk, vt, d, tq=256, group_size=4):
    b, s, _ = qk.shape
    dh = d // N_HEADS
    assert 2 * dh == LANES and s % tq == 0
    npair = d // LANES
    return pl.pallas_call(
        functools.partial(_sb_body, tq=tq, dh=dh, group_size=group_size),
        grid=(b, npair, s // tq),
        in_specs=[pl.BlockSpec((1, tq, LANES), lambda bi, hp, i: (bi, i, hp)),
                  pl.BlockSpec((1, s, LANES), lambda bi, hp, i: (bi, 0, npair + hp)),
                  pl.BlockSpec((1, LANES, s), lambda bi, hp, i: (bi, hp, 0))],
        out_specs=pl.BlockSpec((1, tq, LANES), lambda bi, hp, i: (bi, i, hp)),
        out_shape=jax.ShapeDtypeStruct((b, s, d), BF16),
        scratch_shapes=[pltpu.VMEM((2, LANES, tq), F32)],
        compiler_params=_params("parallel", "parallel", "arbitrary"),
        name="sb_attention",
    )(qk, qk, vt)


def _moba_body(slopes_ref, q_ref, k_ref, vt_ref, km_ref, kp_ref, o_ref,
               acc_ref, m_ref, l_ref, bias_ref, off_ref, ua_ref, ub_ref, *, tq, dh, nb, topk, group_size):
    hp = pl.program_id(1)
    i = pl.program_id(2)
    qh = _head_pair(q_ref[0], dh)
    km = km_ref[0]
    key = lax.broadcasted_iota(jnp.int32, (tq, tq), 0)
    qry = lax.broadcasted_iota(jnp.int32, (tq, tq), 1)
    rel = (qry - key).astype(F32)
    causal = qry >= key
    blk = lax.broadcasted_iota(jnp.int32, (nb, tq), 0).astype(F32)
    own = i.astype(F32)
    start_own = pl.multiple_of(i * tq, tq)
    k_own = k_ref[0, pl.ds(start_own, tq), :]
    vt_own = vt_ref[0, :, pl.ds(start_own, tq)]
    dead = []

    for h in range(2):
        slope = slopes_ref[2 * hp + h]
        gate = lax.dot_general(km, qh[h].astype(F32), NT_DIMS, preferred_element_type=F32,
                               precision=lax.Precision.HIGHEST)
        gate = jnp.where(blk < own, gate, -jnp.inf)
        sel = jnp.zeros((nb, tq), F32)
        for _ in range(topk):
            mx = jnp.max(gate, axis=0, keepdims=True)
            idx = jnp.min(jnp.where(gate == mx, blk, float(nb)), axis=0, keepdims=True)
            pick = blk == idx
            sel = jnp.where(pick & (blk < own), 1.0, sel)
            gate = jnp.where(pick, -jnp.inf, gate)
        off = jnp.where(sel > 0.0, -slope * float(tq) * (own - blk), MASKED)
        for r in range(nb):
            off_ref[h, r] = off[r:r + 1, :]
        bias_ref[h] = -slope * rel
        z = lax.dot_general(k_own, qh[h], NT_DIMS, preferred_element_type=F32)
        s = jnp.where(causal, z + bias_ref[h], MASKED)
        m = jnp.max(s, axis=0, keepdims=True)
        p = jnp.exp2(s - m)
        m_ref[h] = m
        l_ref[h] = jnp.sum(p, axis=0, keepdims=True)
        acc_ref[h] = jnp.dot(vt_own, p.astype(BF16), preferred_element_type=F32)
        qf = qh[h].astype(F32)
        q_norm = jnp.sqrt(jnp.max(jnp.sum(qf * qf, axis=1, keepdims=True), axis=0, keepdims=True))
        m_min = jnp.min(m, axis=1, keepdims=True)
        blk_row = lax.broadcasted_iota(jnp.int32, (1, nb), 1).astype(F32)
        nearest = float(tq) * (own - blk_row) - float(tq - 1)
        bound = q_norm * kp_ref[0, 0, h:h + 1, :] - slope * nearest - m_min
        dead.append((bound <= DEAD_LOG2) & (blk_row < own))

    n_dead = jnp.sum(jnp.where(dead[0] & dead[1], 1.0, 0.0)).astype(jnp.int32)
    heads = range(2)

    def tiles_of(g):
        return [jnp.minimum(group_size * g + t, nb - 1) for t in range(group_size)]

    def logits(g, u_ref):
        for t, j in enumerate(tiles_of(g)):
            k2 = k_ref[0, pl.ds(pl.multiple_of(j * tq, tq), tq), :]
            for h in heads:
                u_ref[h, t] = lax.dot_general(k2, qh[h], NT_DIMS, preferred_element_type=F32) + bias_ref[h]

    def update(g, u_ref):
        js = tiles_of(g)
        offs = [[off_ref[h, j] for j in js] for h in heads]
        m_new, l_new, acc = [], [], []
        for h in heads:
            m_old = m_ref[h]
            mh = m_old
            for t in range(group_size):
                mh = jnp.maximum(mh, jnp.max(u_ref[h, t], axis=0, keepdims=True) + offs[h][t])
            alpha = jnp.exp2(m_old - mh)
            m_new.append(mh)
            l_new.append(alpha * l_ref[h])
            acc.append(alpha * acc_ref[h])
        for t, j in enumerate(js):
            vt = vt_ref[0, :, pl.ds(pl.multiple_of(j * tq, tq), tq)]
            for h in heads:
                p = jnp.exp2(u_ref[h, t] + (offs[h][t] - m_new[h]))
                l_new[h] = l_new[h] + jnp.sum(p, axis=0, keepdims=True)
                acc[h] = acc[h] + jnp.dot(vt, p.astype(BF16), preferred_element_type=F32)
        for h in heads:
            m_ref[h] = m_new[h]
            l_ref[h] = l_new[h]
            acc_ref[h] = acc[h]

    first = n_dead // group_size
    n_groups = (i + group_size - 1) // group_size - first

    @pl.when(n_groups > 0)
    def _():
        logits(first, ua_ref)

    def two_groups(t, carry):
        g = first + 2 * t
        logits(g + 1, ub_ref)
        update(g, ua_ref)
        logits(g + 2, ua_ref)
        update(g + 1, ub_ref)
        return carry

    lax.fori_loop(0, (n_groups + 1) // 2, two_groups, 0)

    row = lax.broadcasted_iota(jnp.int32, (LANES, tq), 0)
    o_t = jnp.where(row < dh, acc_ref[0] / l_ref[0], acc_ref[1] / l_ref[1])
    o_ref[0] = o_t.T.astype(o_ref.dtype)


def _moba_attention(q, k, vt, kmeans, key_norm_bound, slopes, group_size=2):
    b, s, d = q.shape
    dh = d // N_HEADS
    tq = MOBA_BLOCK
    assert 2 * dh == LANES and s % tq == 0
    nb = s // tq
    npair = d // LANES
    return pl.pallas_call(
        functools.partial(_moba_body, tq=tq, dh=dh, nb=nb, topk=min(MOBA_TOPK, nb), group_size=group_size),
        grid=(b, npair, nb),
        in_specs=[pl.BlockSpec(memory_space=pltpu.SMEM),
                  pl.BlockSpec((1, tq, LANES), lambda bi, hp, i: (bi, i, hp)),
                  pl.BlockSpec((1, s, LANES), lambda bi, hp, i: (bi, 0, hp)),
                  pl.BlockSpec((1, LANES, s), lambda bi, hp, i: (bi, hp, 0)),
                  pl.BlockSpec((1, nb, LANES), lambda bi, hp, i: (bi, 0, hp)),
                  pl.BlockSpec((1, 1, 2, nb), lambda bi, hp, i: (bi, hp, 0, 0))],
        out_specs=pl.BlockSpec((1, tq, LANES), lambda bi, hp, i: (bi, i, hp)),
        out_shape=jax.ShapeDtypeStruct((b, s, d), BF16),
        scratch_shapes=[pltpu.VMEM((2, LANES, tq), F32),
                        pltpu.VMEM((2, 1, tq), F32),
                        pltpu.VMEM((2, 1, tq), F32),
                        pltpu.VMEM((2, tq, tq), F32),
                        pltpu.VMEM((2, nb, 1, tq), F32),
                        pltpu.VMEM((2, group_size, tq, tq), F32),
                        pltpu.VMEM((2, group_size, tq, tq), F32)],
        compiler_params=_params("parallel", "parallel", "arbitrary"),
        name="moba_attention",
    )(slopes, q, k, vt, kmeans, key_norm_bound)


def _proj_res_body(o_ref, w_ref, x_ref, gt_ref, out_ref):
    y = jnp.dot(o_ref[0], w_ref[...], preferred_element_type=F32)
    out_ref[0] = x_ref[0] + (1.0 + gt_ref[0]) * y


def _proj_residual(o, w, x, mod, gate_idx, tm=512):
    b, s, d = x.shape
    return pl.pallas_call(
        _proj_res_body,
        grid=(b, s // tm),
        in_specs=[pl.BlockSpec((1, tm, d), lambda bi, i: (bi, i, 0)),
                  pl.BlockSpec((d, d), lambda bi, i: (0, 0)),
                  pl.BlockSpec((1, tm, d), lambda bi, i: (bi, i, 0)),
                  pl.BlockSpec((1, 1, d), lambda bi, i: (bi, 0, gate_idx))],
        out_specs=pl.BlockSpec((1, tm, d), lambda bi, i: (bi, i, 0)),
        out_shape=jax.ShapeDtypeStruct((b, s, d), F32),
        compiler_params=_params("parallel", "parallel"),
        name="proj_residual",
    )(o, w, x, mod)


def _route(logits):
    tm = logits.shape[0]
    lane = lax.broadcasted_iota(jnp.int32, (tm, LANES), 1).astype(F32)
    big = float(LANES)
    is_grp = (lane >= float(N_EXPERTS)) & (lane < float(N_EXPERTS + N_GROUPS))
    lg = jnp.where(is_grp, logits, -jnp.inf)
    mg = jnp.max(lg, axis=-1, keepdims=True)
    p_g = 1.0 / jnp.sum(jnp.exp(lg - mg), axis=-1, keepdims=True)
    g_sel = jnp.min(jnp.where(lg == mg, lane, big), axis=-1, keepdims=True) - float(N_EXPERTS)
    lo = g_sel * float(EXPERTS_PER_GROUP)
    in_grp = (lane >= lo) & (lane < lo + float(EXPERTS_PER_GROUP))
    le = jnp.where(in_grp, logits, -jnp.inf)
    m1 = jnp.max(le, axis=-1, keepdims=True)
    i1 = jnp.min(jnp.where(le == m1, lane, big), axis=-1, keepdims=True)
    le2 = jnp.where(lane == i1, -jnp.inf, le)
    m2 = jnp.max(le2, axis=-1, keepdims=True)
    i2 = jnp.min(jnp.where(le2 == m2, lane, big), axis=-1, keepdims=True)
    e2 = jnp.exp(m2 - m1)
    den = 1.0 + e2
    return jnp.where(lane == i1, p_g / den, 0.0) + jnp.where(lane == i2, p_g * e2 / den, 0.0)


def _moe_body(x_ref, g_ref, sh_ref, sc_ref, gt_ref, wr_ref, br_ref, wg_ref, wu_ref, wd_ref, gf_ref,
              out_ref, h_ref, comb_ref, acc_ref, *, final_norm):
    gi = pl.program_id(1)

    @pl.when(gi == 0)
    def _():
        h = _norm_mod(x_ref[...], g_ref[...], sh_ref[0], sc_ref[0])
        h_ref[...] = h.astype(BF16)
        logits = jnp.dot(h, wr_ref[...], preferred_element_type=F32,
                         precision=lax.Precision.HIGHEST) + br_ref[...]
        comb_ref[...] = _route(logits)
        acc_ref[...] = jnp.zeros_like(acc_ref)

    h = h_ref[...]
    comb = comb_ref[...]
    lane = lax.broadcasted_iota(jnp.int32, comb.shape, 1)
    for e in range(EXPERTS_PER_GROUP):
        ce = jnp.sum(jnp.where(lane == gi * EXPERTS_PER_GROUP + e, comb, 0.0), axis=-1, keepdims=True)
        hg = jnp.dot(h, wg_ref[e], preferred_element_type=F32)
        hu = jnp.dot(h, wu_ref[e], preferred_element_type=F32)
        hid = hg * (1.0 / (1.0 + jnp.exp(-hg))) * hu * ce
        acc_ref[...] += jnp.dot(hid.astype(BF16), wd_ref[e], preferred_element_type=F32)

    @pl.when(gi == N_GROUPS - 1)
    def _():
        y = x_ref[...] + (1.0 + gt_ref[0]) * acc_ref[...]
        if final_norm:
            ms = jnp.mean(y * y, axis=-1, keepdims=True)
            y = y * lax.rsqrt(ms + RMS_EPS) * gf_ref[...]
        out_ref[...] = y


def _moe(x, g, mod, shift_idx, scale_idx, gate_idx, w_router, b_router, w_gate, w_up, w_down, g_final,
         final_norm, tm=512):
    b, s, d = x.shape
    ff = w_gate.shape[2]
    t = b * s
    per_batch = s // tm
    epg = EXPERTS_PER_GROUP
    out = pl.pallas_call(
        functools.partial(_moe_body, final_norm=final_norm),
        grid=(t // tm, N_GROUPS),
        in_specs=[pl.BlockSpec((tm, d), lambda i, gi: (i, 0)),
                  pl.BlockSpec((1, d), lambda i, gi: (0, 0)),
                  pl.BlockSpec((1, 1, d), lambda i, gi: (i // per_batch, 0, shift_idx)),
                  pl.BlockSpec((1, 1, d), lambda i, gi: (i // per_batch, 0, scale_idx)),
                  pl.BlockSpec((1, 1, d), lambda i, gi: (i // per_batch, 0, gate_idx)),
                  pl.BlockSpec((d, LANES), lambda i, gi: (0, 0)),
                  pl.BlockSpec((1, LANES), lambda i, gi: (0, 0)),
                  pl.BlockSpec((epg, d, ff), lambda i, gi: (gi, 0, 0)),
                  pl.BlockSpec((epg, d, ff), lambda i, gi: (gi, 0, 0)),
                  pl.BlockSpec((epg, ff, d), lambda i, gi: (gi, 0, 0)),
                  pl.BlockSpec((1, d), lambda i, gi: (0, 0))],
        out_specs=pl.BlockSpec((tm, d), lambda i, gi: (i, 0)),
        out_shape=jax.ShapeDtypeStruct((t, d), F32),
        scratch_shapes=[pltpu.VMEM((tm, d), BF16),
                        pltpu.VMEM((tm, LANES), F32),
                        pltpu.VMEM((tm, d), F32)],
        compiler_params=_params("parallel", "arbitrary"),
        name="moe_final" if final_norm else "moe",
    )(x.reshape(t, d), g.reshape(1, d), mod, mod, mod, w_router, b_router, w_gate, w_up, w_down,
      g_final.reshape(1, d))
    return out.reshape(b, s, d)


def _router_params(w_rg, b_rg, w_re, b_re):
    d = w_rg.shape[0]
    pad = LANES - N_EXPERTS - N_GROUPS
    w = jnp.concatenate([w_re, w_rg, jnp.zeros((d, pad), F32)], axis=1)
    bias = jnp.concatenate([b_re, b_rg, jnp.zeros((pad,), F32)]).reshape(1, LANES)
    return w, bias


def kernel(x, c, w_ada, b_ada, g_attn, g_ffn, w_qkv_a, w_o_a, w_ada_kv, b_ada_kv, g_kv, w_kv, w_q_b, w_o_b,
           w_rg, b_rg, w_re, b_re, w_gate, w_up, w_down, g_final):
    b, s, d = x.shape
    depth = w_ada.shape[0]
    n_a = depth // 2
    q_scale = LOG2E * (d // N_HEADS) ** -0.5
    slopes = LOG2E * jnp.exp2(-ALIBI_MAX_BIAS * jnp.arange(1, N_HEADS + 1, dtype=F32) / N_HEADS)
    mods = _ada(c, w_ada, b_ada).reshape(depth, b, 1, 6 * d)
    mod_kv = _ada(c, w_ada_kv[None], b_ada_kv[None]).reshape(b, 1, 2 * d)
    k = vt = kmeans = key_norm_bound = None
    for l in range(depth):
        mod = mods[l]
        if l < n_a:
            w_qkv = jnp.concatenate([w_qkv_a[l][:, :d] * q_scale, w_qkv_a[l][:, d:]], axis=1).astype(BF16)
            qk, v_t = _norm_matmul(x, g_attn[l], mod, 0, 1, w_qkv, with_t=True)
            o = _sb_attention(qk, v_t, d)
            w_o = w_o_a[l]
        else:
            j = l - n_a
            q, = _norm_matmul(x, g_attn[l], mod, 0, 1, (w_q_b[j] * q_scale).astype(BF16))
            o = _moba_attention(q, k, vt, kmeans, key_norm_bound, slopes)
            w_o = w_o_b[j]
        x = _proj_residual(o, w_o.astype(BF16), x, mod, 2)
        w_router, b_router = _router_params(w_rg[l], b_rg[l], w_re[l], b_re[l])
        x = _moe(x, g_ffn[l], mod, 3, 4, 5, w_router, b_router, w_gate[l].astype(BF16), w_up[l].astype(BF16),
                 w_down[l].astype(BF16), g_final, final_norm=(l == depth - 1))
        if l == n_a - 1:
            k, vt, kmeans, kn2 = _norm_matmul(x, g_kv, mod_kv, 0, 1, w_kv.astype(BF16), with_t=True,
                                              with_means=True)
            nb = s // MOBA_BLOCK
            kmeans = kmeans.reshape(b, nb, d)
            key_norm = NORM_MARGIN * jnp.sqrt(kn2.reshape(b, nb, LANES)[:, :, :N_HEADS])
            key_norm_bound = lax.cummax(key_norm, axis=1).transpose(0, 2, 1).reshape(b, N_HEADS // 2, 2, nb)
    return x
```

```python
import functools
import math

import jax
import jax.numpy as jnp
from jax import lax
from jax.experimental import pallas as pl
from jax.experimental.pallas import tpu as pltpu

N_HEADS = 16
MOBA_BLOCK = 256
MOBA_TOPK = 3
N_GROUPS = 4
EXPERTS_PER_GROUP = 4
N_EXPERTS = N_GROUPS * EXPERTS_PER_GROUP
RMS_EPS = 1e-6
ALIBI_MAX_BIAS = 8.0

LANES = 128
VMEM_LIMIT = 56 * 1024 * 1024
MASKED = -1e30
LOG2E = math.log2(math.e)
NORM_MARGIN = 1.02
DEAD_LOG2 = -160.0

F32 = jnp.float32
BF16 = jnp.bfloat16
NT_DIMS = (((1,), (1,)), ((), ()))


def _params(*sem):
    return pltpu.CompilerParams(dimension_semantics=sem, vmem_limit_bytes=VMEM_LIMIT)


def _ada_body(c_ref, w_ref, b_ref, o_ref):
    o_ref[0] = jnp.dot(c_ref[...], w_ref[0], preferred_element_type=F32,
                       precision=lax.Precision.HIGHEST) + b_ref[0]


def _ada(c, w, b, tn=512):
    nl, d, n = w.shape
    bsz = c.shape[0]
    return pl.pallas_call(
        _ada_body,
        grid=(nl, n // tn),
        in_specs=[pl.BlockSpec((bsz, d), lambda l, j: (0, 0)),
                  pl.BlockSpec((1, d, tn), lambda l, j: (l, 0, j)),
                  pl.BlockSpec((1, 1, tn), lambda l, j: (l, 0, j))],
        out_specs=pl.BlockSpec((1, bsz, tn), lambda l, j: (l, 0, j)),
        out_shape=jax.ShapeDtypeStruct((nl, bsz, n), F32),
        compiler_params=_params("parallel", "parallel"),
        name="ada_mod",
    )(c, w, b.reshape(nl, 1, n))


def _norm_mod(x, g, shift, scale):
    ms = jnp.mean(x * x, axis=-1, keepdims=True)
    y = x * lax.rsqrt(ms + RMS_EPS) * g
    return y * (1.0 + scale) + shift


def _norm_mm_body(x_ref, g_ref, sh_ref, sc_ref, w_ref, o_ref, *rest, n_plain, d_model, with_t, with_means):
    h = _norm_mod(x_ref[0], g_ref[...], sh_ref[0], sc_ref[0])
    y = jnp.dot(h.astype(BF16), w_ref[...], preferred_element_type=F32)
    o_ref[0] = y[:, :n_plain].astype(o_ref.dtype)
    if with_t:
        rest[0][0] = y[:, n_plain:].T.astype(BF16)
    if with_means:
        km_ref, kn_ref = rest[-2:]
        kb = y[:, :d_model].astype(BF16).astype(F32)
        dcol = lax.broadcasted_iota(jnp.int32, (d_model, LANES), 0)
        head = lax.broadcasted_iota(jnp.int32, (d_model, LANES), 1)
        seg = jnp.where(dcol // (d_model // N_HEADS) == head, 1.0, 0.0).astype(BF16)
        kn2 = jnp.dot((kb * kb).astype(BF16), seg, preferred_element_type=F32)
        for r in range(y.shape[0] // MOBA_BLOCK):
            rows = slice(r * MOBA_BLOCK, (r + 1) * MOBA_BLOCK)
            km_ref[0, r] = jnp.mean(y[rows, :d_model], axis=0, keepdims=True)
            kn_ref[0, r] = jnp.max(kn2[rows], axis=0, keepdims=True)


def _norm_matmul(x, g, mod, shift_idx, scale_idx, w, with_t=False, with_means=False, tm=512):
    b, s, d = x.shape
    n = w.shape[1]
    n_plain = n - d if with_t else n
    out_shape = [jax.ShapeDtypeStruct((b, s, n_plain), BF16)]
    out_specs = [pl.BlockSpec((1, tm, n_plain), lambda bi, i: (bi, i, 0))]
    if with_t:
        out_shape.append(jax.ShapeDtypeStruct((b, d, s), BF16))
        out_specs.append(pl.BlockSpec((1, d, tm), lambda bi, i: (bi, 0, i)))
    if with_means:
        out_shape.append(jax.ShapeDtypeStruct((b, s // MOBA_BLOCK, 1, d), F32))
        out_specs.append(pl.BlockSpec((1, tm // MOBA_BLOCK, 1, d), lambda bi, i: (bi, i, 0, 0)))
        out_shape.append(jax.ShapeDtypeStruct((b, s // MOBA_BLOCK, 1, LANES), F32))
        out_specs.append(pl.BlockSpec((1, tm // MOBA_BLOCK, 1, LANES), lambda bi, i: (bi, i, 0, 0)))
    return pl.pallas_call(
        functools.partial(_norm_mm_body, n_plain=n_plain, d_model=d, with_t=with_t, with_means=with_means),
        grid=(b, s // tm),
        in_specs=[pl.BlockSpec((1, tm, d), lambda bi, i: (bi, i, 0)),
                  pl.BlockSpec((1, d), lambda bi, i: (0, 0)),
                  pl.BlockSpec((1, 1, d), lambda bi, i: (bi, 0, shift_idx)),
                  pl.BlockSpec((1, 1, d), lambda bi, i: (bi, 0, scale_idx)),
                  pl.BlockSpec((d, n), lambda bi, i: (0, 0))],
        out_specs=out_specs,
        out_shape=out_shape,
        compiler_params=_params("parallel", "parallel"),
        name="norm_matmul" + ("_t" if with_t else "") + ("_means" if with_means else ""),
    )(x, g.reshape(1, d), mod, mod, w)


def _head_pair(q2, dh):
    lane = lax.broadcasted_iota(jnp.int32, q2.shape, 1)
    zero = jnp.zeros_like(q2)
    return jnp.where(lane < dh, q2, zero), jnp.where(lane >= dh, q2, zero)


def _sb_body(q_ref, k_ref, vt_ref, o_ref, acc_ref, *, tq, dh, group_size):
    i = pl.program_id(2)
    qh = _head_pair(q_ref[0], dh)
    key = lax.broadcasted_iota(jnp.int32, (tq, tq), 0)
    qry = lax.broadcasted_iota(jnp.int32, (tq, tq), 1)
    past = key < qry
    neg_upper = jnp.where(qry > key, -1.0, 0.0).astype(BF16)
    sign_bit = jnp.uint32(0x80000000)

    acc_ref[...] = jnp.zeros_like(acc_ref)

    def blocks(js, carries, n_diag=0):
        ks = [k_ref[0, pl.ds(pl.multiple_of(j * tq, tq), tq), :] for j in js]
        vts = [vt_ref[0, :, pl.ds(pl.multiple_of(j * tq, tq), tq)] for j in js]
        chains = [(b, h) for b in range(len(js)) for h in range(2)]
        carry = list(carries)
        acc = [acc_ref[0], acc_ref[1]]
        z, lbeta, sp, sp0, excl, w = {}, {}, {}, {}, {}, {}

        def stage(c, s):
            b, h = c
            if s == 0:
                z[c] = lax.dot_general(ks[b], qh[h], NT_DIMS, preferred_element_type=F32)
            elif s == 1:
                zc = z.pop(c)
                neg_abs = lax.bitcast_convert_type(lax.bitcast_convert_type(zc, jnp.uint32) | sign_bit, F32)
                v = jnp.maximum(zc, 0.0) + jnp.log2(1.0 + jnp.exp2(neg_abs))
                lbeta[c] = zc - v
                if b < n_diag:
                    v = jnp.where(past, v, 0.0)
                sp0[c] = v[0:1, :]
                sp[c] = v.astype(BF16)
            elif s == 2:
                excl[c] = jnp.dot(neg_upper, sp.pop(c), preferred_element_type=F32)
            elif s == 3:
                v = jnp.exp2(lbeta.pop(c) + excl[c] + carry[h])
                if b < n_diag:
                    v = jnp.where(past, v, 0.0)
                w[c] = v.astype(BF16)
                carry[h] = carry[h] + excl.pop(c)[0:1, :] - sp0.pop(c)
            else:
                acc[h] = acc[h] + jnp.dot(vts[b], w.pop(c), preferred_element_type=F32)

        n_stages = 5
        for t in range(len(chains) + n_stages - 1):
            for ci, c in enumerate(chains):
                if 0 <= t - ci < n_stages:
                    stage(c, t - ci)
        acc_ref[0] = acc[0]
        acc_ref[1] = acc[1]
        return tuple(carry)

    def alive(carries):
        return (jnp.max(jnp.maximum(carries[0], carries[1])) > DEAD_LOG2).astype(jnp.int32)

    def walk(n, tiles_of, state):
        def body(st):
            carries = blocks(tiles_of(st[0]), (st[2], st[3]))
            return st[0] + 1, alive(carries), carries[0], carries[1]
        st = lax.while_loop(lambda st: (st[0] < n) & (st[1] > 0), body, (jnp.int32(0),) + state)
        return st[1:]

    zeros = jnp.zeros((1, tq), F32)
    carries = lax.cond(i > 0,
                       lambda: blocks([i, i - 1], (zeros, zeros), n_diag=1),
                       lambda: blocks([i], (zeros, zeros), n_diag=1))
    rest = jnp.maximum(i - 1, 0)
    n_single = jnp.where(rest > 0, rest - group_size * ((rest - 1) // group_size), 0)
    state = walk(n_single, lambda t: [i - 2 - t], (alive(carries),) + carries)
    top = i - 2 - n_single
    walk((rest - n_single) // group_size,
         lambda t: [top - group_size * t - g for g in range(group_size)], state)
    row = lax.broadcasted_iota(jnp.int32, (LANES, tq), 0)
    o_ref[0] = jnp.where(row < dh, acc_ref[0], acc_ref[1]).T.astype(o_ref.dtype)


def _sb_attention(qk, vt, d, tq=256, group_size=4):
    b, s, _ = qk.shape
    dh = d // N_HEADS
    assert 2 * dh == LANES and s % tq == 0
    npair = d // LANES
    return pl.pallas_call(
        functools.partial(_sb_body, tq=tq, dh=dh, group_size=group_size),
        grid=(b, npair, s // tq),
        in_specs=[pl.BlockSpec((1, tq, LANES), lambda bi, hp, i: (bi, i, hp)),
                  pl.BlockSpec((1, s, LANES), lambda bi, hp, i: (bi, 0, npair + hp)),
                  pl.BlockSpec((1, LANES, s), lambda bi, hp, i: (bi, hp, 0))],
        out_specs=pl.BlockSpec((1, tq, LANES), lambda bi, hp, i: (bi, i, hp)),
        out_shape=jax.ShapeDtypeStruct((b, s, d), BF16),
        scratch_shapes=[pltpu.VMEM((2, LANES, tq), F32)],
        compiler_params=_params("parallel", "parallel", "arbitrary"),
        name="sb_attention",
    )(qk, qk, vt)


def _moba_body(slopes_ref, q_ref, k_ref, vt_ref, km_ref, kp_ref, o_ref,
               acc_ref, m_ref, l_ref, bias_ref, off_ref, ua_ref, ub_ref, *, tq, dh, nb, topk, group_size):
    hp = pl.program_id(1)
    i = pl.program_id(2)
    qh = _head_pair(q_ref[0], dh)
    km = km_ref[0]
    key = lax.broadcasted_iota(jnp.int32, (tq, tq), 0)
    qry = lax.broadcasted_iota(jnp.int32, (tq, tq), 1)
    rel = (qry - key).astype(F32)
    causal = qry >= key
    blk = lax.broadcasted_iota(jnp.int32, (nb, tq), 0).astype(F32)
    own = i.astype(F32)
    start_own = pl.multiple_of(i * tq, tq)
    k_own = k_ref[0, pl.ds(start_own, tq), :]
    vt_own = vt_ref[0, :, pl.ds(start_own, tq)]
    dead = []

    for h in range(2):
        slope = slopes_ref[2 * hp + h]
        gate = lax.dot_general(km, qh[h].astype(F32), NT_DIMS, preferred_element_type=F32,
                               precision=lax.Precision.HIGHEST)
        gate = jnp.where(blk < own, gate, -jnp.inf)
        sel = jnp.zeros((nb, tq), F32)
        for _ in range(topk):
            mx = jnp.max(gate, axis=0, keepdims=True)
            idx = jnp.min(jnp.where(gate == mx, blk, float(nb)), axis=0, keepdims=True)
            pick = blk == idx
            sel = jnp.where(pick & (blk < own), 1.0, sel)
            gate = jnp.where(pick, -jnp.inf, gate)
        off = jnp.where(sel > 0.0, -slope * float(tq) * (own - blk), MASKED)
        for r in range(nb):
            off_ref[h, r] = off[r:r + 1, :]
        bias_ref[h] = -slope * rel
        z = lax.dot_general(k_own, qh[h], NT_DIMS, preferred_element_type=F32)
        s = jnp.where(causal, z + bias_ref[h], MASKED)
        m = jnp.max(s, axis=0, keepdims=True)
        p = jnp.exp2(s - m)
        m_ref[h] = m
        l_ref[h] = jnp.sum(p, axis=0, keepdims=True)
        acc_ref[h] = jnp.dot(vt_own, p.astype(BF16), preferred_element_type=F32)
        qf = qh[h].astype(F32)
        q_norm = jnp.sqrt(jnp.max(jnp.sum(qf * qf, axis=1, keepdims=True), axis=0, keepdims=True))
        m_min = jnp.min(m, axis=1, keepdims=True)
        blk_row = lax.broadcasted_iota(jnp.int32, (1, nb), 1).astype(F32)
        nearest = float(tq) * (own - blk_row) - float(tq - 1)
        bound = q_norm * kp_ref[0, 0, h:h + 1, :] - slope * nearest - m_min
        dead.append((bound <= DEAD_LOG2) & (blk_row < own))

    n_dead = jnp.sum(jnp.where(dead[0] & dead[1], 1.0, 0.0)).astype(jnp.int32)
    heads = range(2)

    def tiles_of(g):
        return [jnp.minimum(group_size * g + t, nb - 1) for t in range(group_size)]

    def logits(g, u_ref):
        for t, j in enumerate(tiles_of(g)):
            k2 = k_ref[0, pl.ds(pl.multiple_of(j * tq, tq), tq), :]
            for h in heads:
                u_ref[h, t] = lax.dot_general(k2, qh[h], NT_DIMS, preferred_element_type=F32) + bias_ref[h]

    def update(g, u_ref):
        js = tiles_of(g)
        offs = [[off_ref[h, j] for j in js] for h in heads]
        m_new, l_new, acc = [], [], []
        for h in heads:
            m_old = m_ref[h]
            mh = m_old
            for t in range(group_size):
                mh = jnp.maximum(mh, jnp.max(u_ref[h, t], axis=0, keepdims=True) + offs[h][t])
            alpha = jnp.exp2(m_old - mh)
            m_new.append(mh)
            l_new.append(alpha * l_ref[h])
            acc.append(alpha * acc_ref[h])
        for t, j in enumerate(js):
            vt = vt_ref[0, :, pl.ds(pl.multiple_of(j * tq, tq), tq)]
            for h in heads:
                p = jnp.exp2(u_ref[h, t] + (offs[h][t] - m_new[h]))
                l_new[h] = l_new[h] + jnp.sum(p, axis=0, keepdims=True)
                acc[h] = acc[h] + jnp.dot(vt, p.astype(BF16), preferred_element_type=F32)
        for h in heads:
            m_ref[h] = m_new[h]
            l_ref[h] = l_new[h]
            acc_ref[h] = acc[h]

    first = n_dead // group_size
    n_groups = (i + group_size - 1) // group_size - first

    @pl.when(n_groups > 0)
    def _():
        logits(first, ua_ref)

    def two_groups(t, carry):
        g = first + 2 * t
        logits(g + 1, ub_ref)
        update(g, ua_ref)
        logits(g + 2, ua_ref)
        update(g + 1, ub_ref)
        return carry

    lax.fori_loop(0, (n_groups + 1) // 2, two_groups, 0)

    row = lax.broadcasted_iota(jnp.int32, (LANES, tq), 0)
    o_t = jnp.where(row < dh, acc_ref[0] / l_ref[0], acc_ref[1] / l_ref[1])
    o_ref[0] = o_t.T.astype(o_ref.dtype)


def _moba_attention(q, k, vt, kmeans, key_norm_bound, slopes, group_size=2):
    b, s, d = q.shape
    dh = d // N_HEADS
    tq = MOBA_BLOCK
    assert 2 * dh == LANES and s % tq == 0
    nb = s // tq
    npair = d // LANES
    return pl.pallas_call(
        functools.partial(_moba_body, tq=tq, dh=dh, nb=nb, topk=min(MOBA_TOPK, nb), group_size=group_size),
        grid=(b, npair, nb),
        in_specs=[pl.BlockSpec(memory_space=pltpu.SMEM),
                  pl.BlockSpec((1, tq, LANES), lambda bi, hp, i: (bi, i, hp)),
                  pl.BlockSpec((1, s, LANES), lambda bi, hp, i: (bi, 0, hp)),
                  pl.BlockSpec((1, LANES, s), lambda bi, hp, i: (bi, hp, 0)),
                  pl.BlockSpec((1, nb, LANES), lambda bi, hp, i: (bi, 0, hp)),
                  pl.BlockSpec((1, 1, 2, nb), lambda bi, hp, i: (bi, hp, 0, 0))],
        out_specs=pl.BlockSpec((1, tq, LANES), lambda bi, hp, i: (bi, i, hp)),
        out_shape=jax.ShapeDtypeStruct((b, s, d), BF16),
        scratch_shapes=[pltpu.VMEM((2, LANES, tq), F32),
                        pltpu.VMEM((2, 1, tq), F32),
                        pltpu.VMEM((2, 1, tq), F32),
                        pltpu.VMEM((2, tq, tq), F32),
                        pltpu.VMEM((2, nb, 1, tq), F32),
                        pltpu.VMEM((2, group_size, tq, tq), F32),
                        pltpu.VMEM((2, group_size, tq, tq), F32)],
        compiler_params=_params("parallel", "parallel", "arbitrary"),
        name="moba_attention",
    )(slopes, q, k, vt, kmeans, key_norm_bound)


def _proj_res_body(o_ref, w_ref, x_ref, gt_ref, out_ref):
    y = jnp.dot(o_ref[0], w_ref[...], preferred_element_type=F32)
    out_ref[0] = x_ref[0] + (1.0 + gt_ref[0]) * y


def _proj_residual(o, w, x, mod, gate_idx, tm=512):
    b, s, d = x.shape
    return pl.pallas_call(
        _proj_res_body,
        grid=(b, s // tm),
        in_specs=[pl.BlockSpec((1, tm, d), lambda bi, i: (bi, i, 0)),
                  pl.BlockSpec((d, d), lambda bi, i: (0, 0)),
                  pl.BlockSpec((1, tm, d), lambda bi, i: (bi, i, 0)),
                  pl.BlockSpec((1, 1, d), lambda bi, i: (bi, 0, gate_idx))],
        out_specs=pl.BlockSpec((1, tm, d), lambda bi, i: (bi, i, 0)),
        out_shape=jax.ShapeDtypeStruct((b, s, d), F32),
        compiler_params=_params("parallel", "parallel"),
        name="proj_residual",
    )(o, w, x, mod)


def _route(logits):
    tm = logits.shape[0]
    lane = lax.broadcasted_iota(jnp.int32, (tm, LANES), 1).astype(F32)
    big = float(LANES)
    is_grp = (lane >= float(N_EXPERTS)) & (lane < float(N_EXPERTS + N_GROUPS))
    lg = jnp.where(is_grp, logits, -jnp.inf)
    mg = jnp.max(lg, axis=-1, keepdims=True)
    p_g = 1.0 / jnp.sum(jnp.exp(lg - mg), axis=-1, keepdims=True)
    g_sel = jnp.min(jnp.where(lg == mg, lane, big), axis=-1, keepdims=True) - float(N_EXPERTS)
    lo = g_sel * float(EXPERTS_PER_GROUP)
    in_grp = (lane >= lo) & (lane < lo + float(EXPERTS_PER_GROUP))
    le = jnp.where(in_grp, logits, -jnp.inf)
    m1 = jnp.max(le, axis=-1, keepdims=True)
    i1 = jnp.min(jnp.where(le == m1, lane, big), axis=-1, keepdims=True)
    le2 = jnp.where(lane == i1, -jnp.inf, le)
    m2 = jnp.max(le2, axis=-1, keepdims=True)
    i2 = jnp.min(jnp.where(le2 == m2, lane, big), axis=-1, keepdims=True)
    e2 = jnp.exp(m2 - m1)
    den = 1.0 + e2
    return jnp.where(lane == i1, p_g / den, 0.0) + jnp.where(lane == i2, p_g * e2 / den, 0.0)


def _moe_body(x_ref, g_ref, sh_ref, sc_ref, gt_ref, wr_ref, br_ref, wg_ref, wu_ref, wd_ref, gf_ref,
              out_ref, h_ref, comb_ref, acc_ref, *, final_norm):
    gi = pl.program_id(1)

    @pl.when(gi == 0)
    def _():
        h = _norm_mod(x_ref[...], g_ref[...], sh_ref[0], sc_ref[0])
        h_hi = h.astype(BF16)
        h_ref[...] = h_hi
        h_lo = (h - h_hi.astype(F32)).astype(BF16)
        w = wr_ref[...]
        w_hi = w.astype(BF16)
        w_lo = (w - w_hi.astype(F32)).astype(BF16)
        logits = (jnp.dot(h_hi, w_hi, preferred_element_type=F32)
                  + (jnp.dot(h_hi, w_lo, preferred_element_type=F32)
                     + jnp.dot(h_lo, w_hi, preferred_element_type=F32))) + br_ref[...]
        comb_ref[...] = _route(logits)
        acc_ref[...] = jnp.zeros_like(acc_ref)

    h = h_ref[...]
    comb = comb_ref[...]
    lane = lax.broadcasted_iota(jnp.int32, comb.shape, 1)
    for e in range(EXPERTS_PER_GROUP):
        ce = jnp.sum(jnp.where(lane == gi * EXPERTS_PER_GROUP + e, comb, 0.0), axis=-1, keepdims=True)
        hg = jnp.dot(h, wg_ref[e], preferred_element_type=F32)
        hu = jnp.dot(h, wu_ref[e], preferred_element_type=F32)
        hid = hg * (1.0 / (1.0 + jnp.exp(-hg))) * hu * ce
        acc_ref[...] += jnp.dot(hid.astype(BF16), wd_ref[e], preferred_element_type=F32)

    @pl.when(gi == N_GROUPS - 1)
    def _():
        y = x_ref[...] + (1.0 + gt_ref[0]) * acc_ref[...]
        if final_norm:
            ms = jnp.mean(y * y, axis=-1, keepdims=True)
            y = y * lax.rsqrt(ms + RMS_EPS) * gf_ref[...]
        out_ref[...] = y


def _moe(x, g, mod, shift_idx, scale_idx, gate_idx, w_router, b_router, w_gate, w_up, w_down, g_final,
         final_norm, tm=1024):
    b, s, d = x.shape
    ff = w_gate.shape[2]
    t = b * s
    per_batch = s // tm
    epg = EXPERTS_PER_GROUP
    out = pl.pallas_call(
        functools.partial(_moe_body, final_norm=final_norm),
        grid=(t // tm, N_GROUPS),
        in_specs=[pl.BlockSpec((tm, d), lambda i, gi: (i, 0)),
                  pl.BlockSpec((1, d), lambda i, gi: (0, 0)),
                  pl.BlockSpec((1, 1, d), lambda i, gi: (i // per_batch, 0, shift_idx)),
                  pl.BlockSpec((1, 1, d), lambda i, gi: (i // per_batch, 0, scale_idx)),
                  pl.BlockSpec((1, 1, d), lambda i, gi: (i // per_batch, 0, gate_idx)),
                  pl.BlockSpec((d, LANES), lambda i, gi: (0, 0)),
                  pl.BlockSpec((1, LANES), lambda i, gi: (0, 0)),
                  pl.BlockSpec((epg, d, ff), lambda i, gi: (gi, 0, 0)),
                  pl.BlockSpec((epg, d, ff), lambda i, gi: (gi, 0, 0)),
                  pl.BlockSpec((epg, ff, d), lambda i, gi: (gi, 0, 0)),
                  pl.BlockSpec((1, d), lambda i, gi: (0, 0))],
        out_specs=pl.BlockSpec((tm, d), lambda i, gi: (i, 0)),
        out_shape=jax.ShapeDtypeStruct((t, d), F32),
        scratch_shapes=[pltpu.VMEM((tm, d), BF16),
                        pltpu.VMEM((tm, LANES), F32),
                        pltpu.VMEM((tm, d), F32)],
        compiler_params=_params("parallel", "arbitrary"),
        name="moe_final" if final_norm else "moe",
    )(x.reshape(t, d), g.reshape(1, d), mod, mod, mod, w_router, b_router, w_gate, w_up, w_down,
      g_final.reshape(1, d))
    return out.reshape(b, s, d)


def _router_params(w_rg, b_rg, w_re, b_re):
    d = w_rg.shape[0]
    pad = LANES - N_EXPERTS - N_GROUPS
    w = jnp.concatenate([w_re, w_rg, jnp.zeros((d, pad), F32)], axis=1)
    bias = jnp.concatenate([b_re, b_rg, jnp.zeros((pad,), F32)]).reshape(1, LANES)
    return w, bias


def kernel(x, c, w_ada, b_ada, g_attn, g_ffn, w_qkv_a, w_o_a, w_ada_kv, b_ada_kv, g_kv, w_kv, w_q_b, w_o_b,
           w_rg, b_rg, w_re, b_re, w_gate, w_up, w_down, g_final):
    b, s, d = x.shape
    depth = w_ada.shape[0]
    n_a = depth // 2
    q_scale = LOG2E * (d // N_HEADS) ** -0.5
    slopes = LOG2E * jnp.exp2(-ALIBI_MAX_BIAS * jnp.arange(1, N_HEADS + 1, dtype=F32) / N_HEADS)
    mods = _ada(c, w_ada, b_ada).reshape(depth, b, 1, 6 * d)
    mod_kv = _ada(c, w_ada_kv[None], b_ada_kv[None]).reshape(b, 1, 2 * d)
    k = vt = kmeans = key_norm_bound = None
    for l in range(depth):
        mod = mods[l]
        if l < n_a:
            w_qkv = jnp.concatenate([w_qkv_a[l][:, :d] * q_scale, w_qkv_a[l][:, d:]], axis=1).astype(BF16)
            qk, v_t = _norm_matmul(x, g_attn[l], mod, 0, 1, w_qkv, with_t=True)
            o = _sb_attention(qk, v_t, d)
            w_o = w_o_a[l]
        else:
            j = l - n_a
            q, = _norm_matmul(x, g_attn[l], mod, 0, 1, (w_q_b[j] * q_scale).astype(BF16))
            o = _moba_attention(q, k, vt, kmeans, key_norm_bound, slopes)
            w_o = w_o_b[j]
        x = _proj_residual(o, w_o.astype(BF16), x, mod, 2)
        w_router, b_router = _router_params(w_rg[l], b_rg[l], w_re[l], b_re[l])
        x = _moe(x, g_ffn[l], mod, 3, 4, 5, w_router, b_router, w_gate[l].astype(BF16), w_up[l].astype(BF16),
                 w_down[l].astype(BF16), g_final, final_norm=(l == depth - 1))
        if l == n_a - 1:
            k, vt, kmeans, kn2 = _norm_matmul(x, g_kv, mod_kv, 0, 1, w_kv.astype(BF16), with_t=True,
                                              with_means=True)
            nb = s // MOBA_BLOCK
            kmeans = kmeans.reshape(b, nb, d)
            key_norm = NORM_MARGIN * jnp.sqrt(kn2.reshape(b, nb, LANES)[:, :, :N_HEADS])
            key_norm_bound = lax.cummax(key_norm, axis=1).transpose(0, 2, 1).reshape(b, N_HEADS // 2, 2, nb)
    return x
```

```python
import functools
import math

import jax
import jax.numpy as jnp
from jax import lax
from jax.experimental import pallas as pl
from jax.experimental.pallas import tpu as pltpu

N_HEADS = 16
MOBA_BLOCK = 256
MOBA_TOPK = 3
N_GROUPS = 4
EXPERTS_PER_GROUP = 4
N_EXPERTS = N_GROUPS * EXPERTS_PER_GROUP
RMS_EPS = 1e-6
ALIBI_MAX_BIAS = 8.0

LANES = 128
BF16_SUBLANES = 16
VMEM_LIMIT = 56 * 1024 * 1024
MASKED = -1e30
LOG2E = math.log2(math.e)
NORM_MARGIN = 1.02
DEAD_LOG2 = -160.0

F32 = jnp.float32
BF16 = jnp.bfloat16
NT_DIMS = (((1,), (1,)), ((), ()))


def _params(*sem):
    return pltpu.CompilerParams(dimension_semantics=sem, vmem_limit_bytes=VMEM_LIMIT)


def _ada_body(c_ref, w_ref, b_ref, o_ref):
    o_ref[0] = jnp.dot(c_ref[...], w_ref[0], preferred_element_type=F32,
                       precision=lax.Precision.HIGHEST) + b_ref[0]


def _ada(c, w, b, tn=512):
    nl, d, n = w.shape
    bsz = c.shape[0]
    return pl.pallas_call(
        _ada_body,
        grid=(nl, n // tn),
        in_specs=[pl.BlockSpec((bsz, d), lambda l, j: (0, 0)),
                  pl.BlockSpec((1, d, tn), lambda l, j: (l, 0, j)),
                  pl.BlockSpec((1, 1, tn), lambda l, j: (l, 0, j))],
        out_specs=pl.BlockSpec((1, bsz, tn), lambda l, j: (l, 0, j)),
        out_shape=jax.ShapeDtypeStruct((nl, bsz, n), F32),
        compiler_params=_params("parallel", "parallel"),
        name="ada_mod",
    )(c, w, b.reshape(nl, 1, n))


def _norm_mod(x, g, shift, scale):
    ms = jnp.mean(x * x, axis=-1, keepdims=True)
    y = x * lax.rsqrt(ms + RMS_EPS) * g
    return y * (1.0 + scale) + shift


def _norm_mm_body(x_ref, g_ref, sh_ref, sc_ref, w_ref, o_ref, *rest, n_plain, d_model, with_t, with_means):
    h = _norm_mod(x_ref[0], g_ref[...], sh_ref[0], sc_ref[0])
    y = jnp.dot(h.astype(BF16), w_ref[...], preferred_element_type=F32)
    o_ref[0] = y[:, :n_plain].astype(o_ref.dtype)
    if with_t:
        rest[0][0] = y[:, n_plain:].T.astype(BF16)
    if with_means:
        km_ref, kn_ref = rest[-2:]
        kb = y[:, :d_model].astype(BF16).astype(F32)
        dcol = lax.broadcasted_iota(jnp.int32, (d_model, LANES), 0)
        head = lax.broadcasted_iota(jnp.int32, (d_model, LANES), 1)
        seg = jnp.where(dcol // (d_model // N_HEADS) == head, 1.0, 0.0).astype(BF16)
        kn2 = jnp.dot((kb * kb).astype(BF16), seg, preferred_element_type=F32)
        for r in range(y.shape[0] // MOBA_BLOCK):
            rows = slice(r * MOBA_BLOCK, (r + 1) * MOBA_BLOCK)
            km_ref[0, r] = jnp.mean(y[rows, :d_model], axis=0, keepdims=True)
            kn_ref[0, r] = jnp.max(kn2[rows], axis=0, keepdims=True)


def _norm_matmul(x, g, mod, shift_idx, scale_idx, w, with_t=False, with_means=False, tm=512):
    b, s, d = x.shape
    n = w.shape[1]
    n_plain = n - d if with_t else n
    out_shape = [jax.ShapeDtypeStruct((b, s, n_plain), BF16)]
    out_specs = [pl.BlockSpec((1, tm, n_plain), lambda bi, i: (bi, i, 0))]
    if with_t:
        out_shape.append(jax.ShapeDtypeStruct((b, d, s), BF16))
        out_specs.append(pl.BlockSpec((1, d, tm), lambda bi, i: (bi, 0, i)))
    if with_means:
        out_shape.append(jax.ShapeDtypeStruct((b, s // MOBA_BLOCK, 1, d), F32))
        out_specs.append(pl.BlockSpec((1, tm // MOBA_BLOCK, 1, d), lambda bi, i: (bi, i, 0, 0)))
        out_shape.append(jax.ShapeDtypeStruct((b, s // MOBA_BLOCK, 1, LANES), F32))
        out_specs.append(pl.BlockSpec((1, tm // MOBA_BLOCK, 1, LANES), lambda bi, i: (bi, i, 0, 0)))
    return pl.pallas_call(
        functools.partial(_norm_mm_body, n_plain=n_plain, d_model=d, with_t=with_t, with_means=with_means),
        grid=(b, s // tm),
        in_specs=[pl.BlockSpec((1, tm, d), lambda bi, i: (bi, i, 0)),
                  pl.BlockSpec((1, d), lambda bi, i: (0, 0)),
                  pl.BlockSpec((1, 1, d), lambda bi, i: (bi, 0, shift_idx)),
                  pl.BlockSpec((1, 1, d), lambda bi, i: (bi, 0, scale_idx)),
                  pl.BlockSpec((d, n), lambda bi, i: (0, 0))],
        out_specs=out_specs,
        out_shape=out_shape,
        compiler_params=_params("parallel", "parallel"),
        name="norm_matmul" + ("_t" if with_t else "") + ("_means" if with_means else ""),
    )(x, g.reshape(1, d), mod, mod, w)


def _head_pair(q2, dh):
    lane = lax.broadcasted_iota(jnp.int32, q2.shape, 1)
    zero = jnp.zeros_like(q2)
    return jnp.where(lane < dh, q2, zero), jnp.where(lane >= dh, q2, zero)


def _sb_body(q_ref, k_ref, vt_ref, o_ref, acc_ref, *, tq, dh, group_size):
    i = pl.program_id(2)
    qh = _head_pair(q_ref[0], dh)
    key = lax.broadcasted_iota(jnp.int32, (tq, tq), 0)
    qry = lax.broadcasted_iota(jnp.int32, (tq, tq), 1)
    past = key < qry
    neg_upper = jnp.where(qry > key, -1.0, 0.0).astype(BF16)
    sign_bit = jnp.uint32(0x80000000)

    acc_ref[...] = jnp.zeros_like(acc_ref)

    def blocks(js, carries, n_diag=0):
        ks = [k_ref[0, pl.ds(pl.multiple_of(j * tq, tq), tq), :] for j in js]
        vts = [vt_ref[0, :, pl.ds(pl.multiple_of(j * tq, tq), tq)] for j in js]
        chains = [(b, h) for b in range(len(js)) for h in range(2)]
        carry = list(carries)
        acc = [acc_ref[0], acc_ref[1]]
        z, lbeta, sp, sp0, excl, w = {}, {}, {}, {}, {}, {}

        def stage(c, s):
            b, h = c
            if s == 0:
                z[c] = lax.dot_general(ks[b], qh[h], NT_DIMS, preferred_element_type=F32)
            elif s == 1:
                zc = z.pop(c)
                neg_abs = lax.bitcast_convert_type(lax.bitcast_convert_type(zc, jnp.uint32) | sign_bit, F32)
                v = jnp.maximum(zc, 0.0) + jnp.log2(1.0 + jnp.exp2(neg_abs))
                lbeta[c] = zc - v
                if b < n_diag:
                    v = jnp.where(past, v, 0.0)
                sp0[c] = v[0:1, :]
                sp[c] = v.astype(BF16)
            elif s == 2:
                excl[c] = jnp.dot(neg_upper, sp.pop(c), preferred_element_type=F32)
            elif s == 3:
                v = jnp.exp2((lbeta.pop(c) + excl[c] + carry[h]).astype(BF16))
                if b < n_diag:
                    v = jnp.where(past, v, jnp.zeros_like(v))
                w[c] = v
                carry[h] = carry[h] + excl.pop(c)[0:1, :] - sp0.pop(c)
            else:
                vt_h = vts[b][h * dh:(h + 1) * dh, :]
                acc[h] = acc[h] + jnp.dot(vt_h, w.pop(c), preferred_element_type=F32)

        n_stages = 5
        for t in range(len(chains) + n_stages - 1):
            for ci, c in enumerate(chains):
                if 0 <= t - ci < n_stages:
                    stage(c, t - ci)
        acc_ref[0] = acc[0]
        acc_ref[1] = acc[1]
        return tuple(carry)

    def alive(carries):
        return (jnp.max(jnp.maximum(carries[0], carries[1])) > DEAD_LOG2).astype(jnp.int32)

    def walk(n, tiles_of, state):
        def body(st):
            carries = blocks(tiles_of(st[0]), (st[2], st[3]))
            return st[0] + 1, alive(carries), carries[0], carries[1]
        st = lax.while_loop(lambda st: (st[0] < n) & (st[1] > 0), body, (jnp.int32(0),) + state)
        return st[1:]

    zeros = jnp.zeros((1, tq), F32)
    carries = lax.cond(i > 0,
                       lambda: blocks([i, i - 1], (zeros, zeros), n_diag=1),
                       lambda: blocks([i], (zeros, zeros), n_diag=1))
    rest = jnp.maximum(i - 1, 0)
    n_single = jnp.where(rest > 0, rest - group_size * ((rest - 1) // group_size), 0)
    state = walk(n_single, lambda t: [i - 2 - t], (alive(carries),) + carries)
    top = i - 2 - n_single
    walk((rest - n_single) // group_size,
         lambda t: [top - group_size * t - g for g in range(group_size)], state)
    o_ref[0] = jnp.concatenate([acc_ref[0], acc_ref[1]], axis=0).T.astype(o_ref.dtype)


def _sb_attention(qk, vt, d, tq=256, group_size=4):
    b, s, _ = qk.shape
    dh = d // N_HEADS
    assert 2 * dh == LANES and s % tq == 0
    npair = d // LANES
    return pl.pallas_call(
        functools.partial(_sb_body, tq=tq, dh=dh, group_size=group_size),
        grid=(b, npair, s // tq),
        in_specs=[pl.BlockSpec((1, tq, LANES), lambda bi, hp, i: (bi, i, hp)),
                  pl.BlockSpec((1, s, LANES), lambda bi, hp, i: (bi, 0, npair + hp)),
                  pl.BlockSpec((1, LANES, s), lambda bi, hp, i: (bi, hp, 0))],
        out_specs=pl.BlockSpec((1, tq, LANES), lambda bi, hp, i: (bi, i, hp)),
        out_shape=jax.ShapeDtypeStruct((b, s, d), BF16),
        scratch_shapes=[pltpu.VMEM((2, dh, tq), F32)],
        compiler_params=_params("parallel", "parallel", "arbitrary"),
        name="sb_attention",
    )(qk, qk, vt)


def _moba_body(slopes_ref, q_ref, k_ref, vt_ref, km_ref, kp_ref, o_ref,
               acc_ref, m_ref, bias_ref, off_ref, ua_ref, ub_ref, vta_ref, *, tq, dh, nb, topk, group_size):
    hp = pl.program_id(1)
    i = pl.program_id(2)

    @pl.when(i == 0)
    def _():
        row = lax.broadcasted_iota(jnp.int32, (BF16_SUBLANES, vta_ref.shape[2]), 0)
        tail = jnp.where(row == 0, 1.0, 0.0).astype(BF16)
        for h in range(2):
            vta_ref[h, :dh, :] = vt_ref[0, h * dh:(h + 1) * dh, :]
            vta_ref[h, dh:, :] = tail

    qh = _head_pair(q_ref[0], dh)
    km = km_ref[0]
    key = lax.broadcasted_iota(jnp.int32, (tq, tq), 0)
    qry = lax.broadcasted_iota(jnp.int32, (tq, tq), 1)
    rel = (qry - key).astype(F32)
    causal = qry >= key
    blk = lax.broadcasted_iota(jnp.int32, (nb, tq), 0).astype(F32)
    own = i.astype(F32)
    start_own = pl.multiple_of(i * tq, tq)
    k_own = k_ref[0, pl.ds(start_own, tq), :]
    heads = range(2)
    slope = [slopes_ref[2 * hp + h] for h in heads]

    blk_row = lax.broadcasted_iota(jnp.int32, (1, nb), 1).astype(F32)
    nearest = float(tq) * (own - blk_row) - float(tq - 1)
    k_own_f = k_own.astype(F32)
    dead = blk_row < own
    for h in heads:
        qf = qh[h].astype(F32)
        q_norm = jnp.sqrt(jnp.max(jnp.sum(qf * qf, axis=1, keepdims=True), axis=0, keepdims=True))
        m_min = jnp.min(jnp.sum(qf * k_own_f, axis=1, keepdims=True), axis=0, keepdims=True)
        bound = q_norm * kp_ref[0, 0, h:h + 1, :] - slope[h] * nearest - m_min
        dead = dead & (bound <= DEAD_LOG2)
    n_dead = jnp.sum(jnp.where(dead, 1.0, 0.0)).astype(jnp.int32)

    def tiles_of(g):
        return [jnp.minimum(group_size * g + t, nb - 1) for t in range(group_size)]

    def logits(g, u_ref):
        for t, j in enumerate(tiles_of(g)):
            k2 = k_ref[0, pl.ds(pl.multiple_of(j * tq, tq), tq), :]
            for h in heads:
                u_ref[h, t] = lax.dot_general(k2, qh[h], NT_DIMS, preferred_element_type=F32) + bias_ref[h]

    for h in heads:
        bias_ref[h] = -slope[h] * rel

    km_hi = km.astype(BF16)
    km_mid = (km - km_hi.astype(F32)).astype(BF16)
    km_lo = (km - km_hi.astype(F32) - km_mid.astype(F32)).astype(BF16)
    for h in heads:
        gate = sum(lax.dot_general(part, qh[h], NT_DIMS, preferred_element_type=F32)
                   for part in (km_lo, km_mid, km_hi))
        gate = jnp.where(blk < own, gate, -jnp.inf)
        sel = jnp.zeros((nb, tq), F32)
        for _ in range(topk):
            mx = jnp.max(gate, axis=0, keepdims=True)
            idx = jnp.min(jnp.where(gate == mx, blk, float(nb)), axis=0, keepdims=True)
            pick = blk == idx
            sel = jnp.where(pick & (blk < own), 1.0, sel)
            gate = jnp.where(pick, -jnp.inf, gate)
        off = jnp.where(sel > 0.0, -slope[h] * float(tq) * (own - blk), MASKED)
        for r in range(nb):
            off_ref[h, r] = off[r:r + 1, :]
        z = lax.dot_general(k_own, qh[h], NT_DIMS, preferred_element_type=F32)
        s = jnp.where(causal, z + bias_ref[h], MASKED)
        m = jnp.max(s, axis=0, keepdims=True)
        p = jnp.exp2((s - m).astype(BF16))
        m_ref[h] = m
        acc_ref[h] = jnp.dot(vta_ref[h, :, pl.ds(start_own, tq)], p, preferred_element_type=F32)

    first = n_dead // group_size
    logits(first, ua_ref)

    def update(g, u_ref):
        js = tiles_of(g)
        offs = [[off_ref[h, j] for j in js] for h in heads]
        m_new, acc = [], []
        for h in heads:
            m_old = m_ref[h]
            mh = m_old
            for t in range(group_size):
                mh = jnp.maximum(mh, jnp.max(u_ref[h, t], axis=0, keepdims=True) + offs[h][t])
            m_new.append(mh)
            acc.append(jnp.exp2(m_old - mh) * acc_ref[h])
        for t, j in enumerate(js):
            start = pl.multiple_of(j * tq, tq)
            for h in heads:
                p = jnp.exp2((u_ref[h, t] + (offs[h][t] - m_new[h])).astype(BF16))
                acc[h] = acc[h] + jnp.dot(vta_ref[h, :, pl.ds(start, tq)], p, preferred_element_type=F32)
        for h in heads:
            m_ref[h] = m_new[h]
            acc_ref[h] = acc[h]

    n_groups = (i + group_size - 1) // group_size - first

    def two_groups(t, carry):
        g = first + 2 * t
        logits(g + 1, ub_ref)
        update(g, ua_ref)
        logits(g + 2, ua_ref)
        update(g + 1, ub_ref)
        return carry

    lax.fori_loop(0, (n_groups + 1) // 2, two_groups, 0)

    o_t = jnp.concatenate([acc_ref[h, :dh, :] / acc_ref[h, dh:dh + 1, :] for h in heads], axis=0)
    o_ref[0] = o_t.T.astype(o_ref.dtype)


def _moba_attention(q, k, vt, kmeans, key_norm_bound, slopes, group_size=2):
    b, s, d = q.shape
    dh = d // N_HEADS
    tq = MOBA_BLOCK
    assert 2 * dh == LANES and s % tq == 0
    nb = s // tq
    npair = d // LANES
    return pl.pallas_call(
        functools.partial(_moba_body, tq=tq, dh=dh, nb=nb, topk=min(MOBA_TOPK, nb), group_size=group_size),
        grid=(b, npair, nb),
        in_specs=[pl.BlockSpec(memory_space=pltpu.SMEM),
                  pl.BlockSpec((1, tq, LANES), lambda bi, hp, i: (bi, i, hp)),
                  pl.BlockSpec((1, s, LANES), lambda bi, hp, i: (bi, 0, hp)),
                  pl.BlockSpec((1, LANES, s), lambda bi, hp, i: (bi, hp, 0)),
                  pl.BlockSpec((1, nb, LANES), lambda bi, hp, i: (bi, 0, hp)),
                  pl.BlockSpec((1, 1, 2, nb), lambda bi, hp, i: (bi, hp, 0, 0))],
        out_specs=pl.BlockSpec((1, tq, LANES), lambda bi, hp, i: (bi, i, hp)),
        out_shape=jax.ShapeDtypeStruct((b, s, d), BF16),
        scratch_shapes=[pltpu.VMEM((2, dh + BF16_SUBLANES, tq), F32),
                        pltpu.VMEM((2, 1, tq), F32),
                        pltpu.VMEM((2, tq, tq), F32),
                        pltpu.VMEM((2, nb, 1, tq), F32),
                        pltpu.VMEM((2, group_size, tq, tq), F32),
                        pltpu.VMEM((2, group_size, tq, tq), F32),
                        pltpu.VMEM((2, dh + BF16_SUBLANES, s), BF16)],
        compiler_params=_params("arbitrary", "arbitrary", "arbitrary"),
        name="moba_attention",
    )(slopes, q, k, vt, kmeans, key_norm_bound)


def _proj_res_body(o_ref, w_ref, x_ref, gt_ref, out_ref):
    y = jnp.dot(o_ref[0], w_ref[...], preferred_element_type=F32)
    out_ref[0] = x_ref[0] + (1.0 + gt_ref[0]) * y


def _proj_residual(o, w, x, mod, gate_idx, tm=512):
    b, s, d = x.shape
    return pl.pallas_call(
        _proj_res_body,
        grid=(b, s // tm),
        in_specs=[pl.BlockSpec((1, tm, d), lambda bi, i: (bi, i, 0)),
                  pl.BlockSpec((d, d), lambda bi, i: (0, 0)),
                  pl.BlockSpec((1, tm, d), lambda bi, i: (bi, i, 0)),
                  pl.BlockSpec((1, 1, d), lambda bi, i: (bi, 0, gate_idx))],
        out_specs=pl.BlockSpec((1, tm, d), lambda bi, i: (bi, i, 0)),
        out_shape=jax.ShapeDtypeStruct((b, s, d), F32),
        compiler_params=_params("parallel", "parallel"),
        name="proj_residual",
    )(o, w, x, mod)


def _route(logits):
    tm = logits.shape[0]
    lane = lax.broadcasted_iota(jnp.int32, (tm, LANES), 1).astype(F32)
    big = float(LANES)
    is_grp = (lane >= float(N_EXPERTS)) & (lane < float(N_EXPERTS + N_GROUPS))
    lg = jnp.where(is_grp, logits, -jnp.inf)
    mg = jnp.max(lg, axis=-1, keepdims=True)
    p_g = 1.0 / jnp.sum(jnp.exp(lg - mg), axis=-1, keepdims=True)
    g_sel = jnp.min(jnp.where(lg == mg, lane, big), axis=-1, keepdims=True) - float(N_EXPERTS)
    lo = g_sel * float(EXPERTS_PER_GROUP)
    in_grp = (lane >= lo) & (lane < lo + float(EXPERTS_PER_GROUP))
    le = jnp.where(in_grp, logits, -jnp.inf)
    m1 = jnp.max(le, axis=-1, keepdims=True)
    i1 = jnp.min(jnp.where(le == m1, lane, big), axis=-1, keepdims=True)
    le2 = jnp.where(lane == i1, -jnp.inf, le)
    m2 = jnp.max(le2, axis=-1, keepdims=True)
    i2 = jnp.min(jnp.where(le2 == m2, lane, big), axis=-1, keepdims=True)
    e2 = jnp.exp(m2 - m1)
    den = 1.0 + e2
    return jnp.where(lane == i1, p_g / den, 0.0) + jnp.where(lane == i2, p_g * e2 / den, 0.0)


def _moe_body(x_ref, g_ref, sh_ref, sc_ref, gt_ref, wr_ref, br_ref, wg_ref, wu_ref, wd_ref, gf_ref,
              out_ref, h_ref, comb_ref, acc_ref, *, final_norm):
    gi = pl.program_id(1)

    @pl.when(gi == 0)
    def _():
        h = _norm_mod(x_ref[...], g_ref[...], sh_ref[0], sc_ref[0])
        h_hi = h.astype(BF16)
        h_ref[...] = h_hi
        h_lo = (h - h_hi.astype(F32)).astype(BF16)
        w = wr_ref[...]
        w_hi = w.astype(BF16)
        w_lo = (w - w_hi.astype(F32)).astype(BF16)
        logits = (jnp.dot(h_hi, w_hi, preferred_element_type=F32)
                  + (jnp.dot(h_hi, w_lo, preferred_element_type=F32)
                     + jnp.dot(h_lo, w_hi, preferred_element_type=F32))) + br_ref[...]
        comb_ref[...] = _route(logits)
        acc_ref[...] = jnp.zeros_like(acc_ref)

    h = h_ref[...]
    comb = comb_ref[...]
    lane = lax.broadcasted_iota(jnp.int32, comb.shape, 1)
    for e in range(EXPERTS_PER_GROUP):
        ce = jnp.sum(jnp.where(lane == gi * EXPERTS_PER_GROUP + e, comb, 0.0), axis=-1, keepdims=True)
        hg = jnp.dot(h, wg_ref[e], preferred_element_type=F32)
        hu = jnp.dot(h, wu_ref[e], preferred_element_type=F32)
        hid = hg * (1.0 / (1.0 + jnp.exp(-hg))) * hu * ce
        acc_ref[...] += jnp.dot(hid.astype(BF16), wd_ref[e], preferred_element_type=F32)

    @pl.when(gi == N_GROUPS - 1)
    def _():
        y = x_ref[...] + (1.0 + gt_ref[0]) * acc_ref[...]
        if final_norm:
            ms = jnp.mean(y * y, axis=-1, keepdims=True)
            y = y * lax.rsqrt(ms + RMS_EPS) * gf_ref[...]
        out_ref[...] = y


def _moe(x, g, mod, shift_idx, scale_idx, gate_idx, w_router, b_router, w_gate, w_up, w_down, g_final,
         final_norm, tm=1024):
    b, s, d = x.shape
    ff = w_gate.shape[2]
    t = b * s
    per_batch = s // tm
    epg = EXPERTS_PER_GROUP
    out = pl.pallas_call(
        functools.partial(_moe_body, final_norm=final_norm),
        grid=(t // tm, N_GROUPS),
        in_specs=[pl.BlockSpec((tm, d), lambda i, gi: (i, 0)),
                  pl.BlockSpec((1, d), lambda i, gi: (0, 0)),
                  pl.BlockSpec((1, 1, d), lambda i, gi: (i // per_batch, 0, shift_idx)),
                  pl.BlockSpec((1, 1, d), lambda i, gi: (i // per_batch, 0, scale_idx)),
                  pl.BlockSpec((1, 1, d), lambda i, gi: (i // per_batch, 0, gate_idx)),
                  pl.BlockSpec((d, LANES), lambda i, gi: (0, 0)),
                  pl.BlockSpec((1, LANES), lambda i, gi: (0, 0)),
                  pl.BlockSpec((epg, d, ff), lambda i, gi: (gi, 0, 0)),
                  pl.BlockSpec((epg, d, ff), lambda i, gi: (gi, 0, 0)),
                  pl.BlockSpec((epg, ff, d), lambda i, gi: (gi, 0, 0)),
                  pl.BlockSpec((1, d), lambda i, gi: (0, 0))],
        out_specs=pl.BlockSpec((tm, d), lambda i, gi: (i, 0)),
        out_shape=jax.ShapeDtypeStruct((t, d), F32),
        scratch_shapes=[pltpu.VMEM((tm, d), BF16),
                        pltpu.VMEM((tm, LANES), F32),
                        pltpu.VMEM((tm, d), F32)],
        compiler_params=_params("parallel", "arbitrary"),
        name="moe_final" if final_norm else "moe",
    )(x.reshape(t, d), g.reshape(1, d), mod, mod, mod, w_router, b_router, w_gate, w_up, w_down,
      g_final.reshape(1, d))
    return out.reshape(b, s, d)


def _router_params(w_rg, b_rg, w_re, b_re):
    d = w_rg.shape[0]
    pad = LANES - N_EXPERTS - N_GROUPS
    w = jnp.concatenate([w_re, w_rg, jnp.zeros((d, pad), F32)], axis=1)
    bias = jnp.concatenate([b_re, b_rg, jnp.zeros((pad,), F32)]).reshape(1, LANES)
    return w, bias


def kernel(x, c, w_ada, b_ada, g_attn, g_ffn, w_qkv_a, w_o_a, w_ada_kv, b_ada_kv, g_kv, w_kv, w_q_b, w_o_b,
           w_rg, b_rg, w_re, b_re, w_gate, w_up, w_down, g_final):
    b, s, d = x.shape
    depth = w_ada.shape[0]
    n_a = depth // 2
    q_scale = LOG2E * (d // N_HEADS) ** -0.5
    slopes = LOG2E * jnp.exp2(-ALIBI_MAX_BIAS * jnp.arange(1, N_HEADS + 1, dtype=F32) / N_HEADS)
    mods = _ada(c, w_ada, b_ada).reshape(depth, b, 1, 6 * d)
    mod_kv = _ada(c, w_ada_kv[None], b_ada_kv[None]).reshape(b, 1, 2 * d)
    k = vt = kmeans = key_norm_bound = None
    for l in range(depth):
        mod = mods[l]
        if l < n_a:
            w_qkv = jnp.concatenate([w_qkv_a[l][:, :d] * q_scale, w_qkv_a[l][:, d:]], axis=1).astype(BF16)
            qk, v_t = _norm_matmul(x, g_attn[l], mod, 0, 1, w_qkv, with_t=True)
            o = _sb_attention(qk, v_t, d)
            w_o = w_o_a[l]
        else:
            j = l - n_a
            q, = _norm_matmul(x, g_attn[l], mod, 0, 1, (w_q_b[j] * q_scale).astype(BF16))
            o = _moba_attention(q, k, vt, kmeans, key_norm_bound, slopes)
            w_o = w_o_b[j]
        x = _proj_residual(o, w_o.astype(BF16), x, mod, 2)
        w_router, b_router = _router_params(w_rg[l], b_rg[l], w_re[l], b_re[l])
        x = _moe(x, g_ffn[l], mod, 3, 4, 5, w_router, b_router, w_gate[l].astype(BF16), w_up[l].astype(BF16),
                 w_down[l].astype(BF16), g_final, final_norm=(l == depth - 1))
        if l == n_a - 1:
            k, vt, kmeans, kn2 = _norm_matmul(x, g_kv, mod_kv, 0, 1, w_kv.astype(BF16), with_t=True,
                                              with_means=True)
            nb = s // MOBA_BLOCK
            kmeans = kmeans.reshape(b, nb, d)
            key_norm = NORM_MARGIN * jnp.sqrt(kn2.reshape(b, nb, LANES)[:, :, :N_HEADS])
            key_norm_bound = lax.cummax(key_norm, axis=1).transpose(0, 2, 1).reshape(b, N_HEADS // 2, 2, nb)
    return x
```

```python
import functools
import math

import jax
import jax.numpy as jnp
from jax import lax
from jax.experimental import pallas as pl
from jax.experimental.pallas import tpu as pltpu

N_HEADS = 16
MOBA_BLOCK = 256
MOBA_TOPK = 3
N_GROUPS = 4
EXPERTS_PER_GROUP = 4
N_EXPERTS = N_GROUPS * EXPERTS_PER_GROUP
RMS_EPS = 1e-6
ALIBI_MAX_BIAS = 8.0

LANES = 128
BF16_SUBLANES = 16
VMEM_LIMIT = 56 * 1024 * 1024
MASKED = -1e30
LOG2E = math.log2(math.e)
NORM_MARGIN = 1.02
DEAD_LOG2 = -160.0

F32 = jnp.float32
BF16 = jnp.bfloat16
NT_DIMS = (((1,), (1,)), ((), ()))


def _params(*sem):
    return pltpu.CompilerParams(dimension_semantics=sem, vmem_limit_bytes=VMEM_LIMIT)


def _ada_body(c_ref, w_ref, b_ref, o_ref):
    o_ref[0] = jnp.dot(c_ref[...], w_ref[0], preferred_element_type=F32,
                       precision=lax.Precision.HIGHEST) + b_ref[0]


def _ada(c, w, b, tn=512):
    nl, d, n = w.shape
    bsz = c.shape[0]
    return pl.pallas_call(
        _ada_body,
        grid=(nl, n // tn),
        in_specs=[pl.BlockSpec((bsz, d), lambda l, j: (0, 0)),
                  pl.BlockSpec((1, d, tn), lambda l, j: (l, 0, j)),
                  pl.BlockSpec((1, 1, tn), lambda l, j: (l, 0, j))],
        out_specs=pl.BlockSpec((1, bsz, tn), lambda l, j: (l, 0, j)),
        out_shape=jax.ShapeDtypeStruct((nl, bsz, n), F32),
        compiler_params=_params("parallel", "parallel"),
        name="ada_mod",
    )(c, w, b.reshape(nl, 1, n))


def _norm_mod(x, g, shift, scale):
    ms = jnp.mean(x * x, axis=-1, keepdims=True)
    y = x * lax.rsqrt(ms + RMS_EPS) * g
    return y * (1.0 + scale) + shift


def _norm_mm_body(x_ref, g_ref, sh_ref, sc_ref, w_ref, o_ref, *rest, n_plain, d_model, with_t, with_means,
                  with_norms):
    h = _norm_mod(x_ref[0], g_ref[...], sh_ref[0], sc_ref[0])
    y = jnp.dot(h.astype(BF16), w_ref[...], preferred_element_type=F32)
    o_ref[0] = y[:, :n_plain].astype(o_ref.dtype)
    if with_t:
        rest[0][0] = y[:, n_plain:].T.astype(BF16)
    n_blocks = y.shape[0] // MOBA_BLOCK
    if with_means:
        km_ref = rest[1 if with_t else 0]
        for r in range(n_blocks):
            km_ref[0, r] = jnp.mean(y[r * MOBA_BLOCK:(r + 1) * MOBA_BLOCK, :d_model], axis=0, keepdims=True)
    if with_norms:
        yb = y[:, :d_model].astype(BF16).astype(F32)
        dcol = lax.broadcasted_iota(jnp.int32, (d_model, LANES), 0)
        head = lax.broadcasted_iota(jnp.int32, (d_model, LANES), 1)
        seg = jnp.where(dcol // (d_model // N_HEADS) == head, 1.0, 0.0).astype(BF16)
        n2 = jnp.dot((yb * yb).astype(BF16), seg, preferred_element_type=F32)
        for r in range(n_blocks):
            rest[-1][0, r] = jnp.max(n2[r * MOBA_BLOCK:(r + 1) * MOBA_BLOCK], axis=0, keepdims=True)


def _norm_matmul(x, g, mod, shift_idx, scale_idx, w, with_t=False, with_means=False, with_norms=False, tm=512):
    b, s, d = x.shape
    n = w.shape[1]
    n_plain = n - d if with_t else n
    out_shape = [jax.ShapeDtypeStruct((b, s, n_plain), BF16)]
    out_specs = [pl.BlockSpec((1, tm, n_plain), lambda bi, i: (bi, i, 0))]
    if with_t:
        out_shape.append(jax.ShapeDtypeStruct((b, d, s), BF16))
        out_specs.append(pl.BlockSpec((1, d, tm), lambda bi, i: (bi, 0, i)))
    if with_means:
        out_shape.append(jax.ShapeDtypeStruct((b, s // MOBA_BLOCK, 1, d), F32))
        out_specs.append(pl.BlockSpec((1, tm // MOBA_BLOCK, 1, d), lambda bi, i: (bi, i, 0, 0)))
    if with_norms:
        out_shape.append(jax.ShapeDtypeStruct((b, s // MOBA_BLOCK, 1, LANES), F32))
        out_specs.append(pl.BlockSpec((1, tm // MOBA_BLOCK, 1, LANES), lambda bi, i: (bi, i, 0, 0)))
    return pl.pallas_call(
        functools.partial(_norm_mm_body, n_plain=n_plain, d_model=d, with_t=with_t, with_means=with_means,
                          with_norms=with_norms),
        grid=(b, s // tm),
        in_specs=[pl.BlockSpec((1, tm, d), lambda bi, i: (bi, i, 0)),
                  pl.BlockSpec((1, d), lambda bi, i: (0, 0)),
                  pl.BlockSpec((1, 1, d), lambda bi, i: (bi, 0, shift_idx)),
                  pl.BlockSpec((1, 1, d), lambda bi, i: (bi, 0, scale_idx)),
                  pl.BlockSpec((d, n), lambda bi, i: (0, 0))],
        out_specs=out_specs,
        out_shape=out_shape,
        compiler_params=_params("parallel", "parallel"),
        name="norm_matmul" + ("_t" if with_t else "") + ("_means" if with_means else "") + ("_norms" if with_norms else ""),
    )(x, g.reshape(1, d), mod, mod, w)


def _per_head(q2, dh):
    lane = lax.broadcasted_iota(jnp.int32, (q2.shape[0], LANES), 1)
    out = []
    for t in range(q2.shape[1] // LANES):
        tile = q2[:, t * LANES:(t + 1) * LANES]
        zero = jnp.zeros_like(tile)
        for half in range(LANES // dh):
            mine = (lane >= half * dh) & (lane < (half + 1) * dh)
            out.append((jnp.where(mine, tile, zero), t))
    return out


def _sb_body(q_ref, k_ref, vt_ref, o_ref, acc_ref, *, tq, dh, group_size):
    i = pl.program_id(2)
    qh = _per_head(q_ref[0], dh)
    heads = range(len(qh))
    key = lax.broadcasted_iota(jnp.int32, (tq, tq), 0)
    qry = lax.broadcasted_iota(jnp.int32, (tq, tq), 1)
    past = key < qry
    neg_upper = jnp.where(qry > key, -1.0, 0.0).astype(BF16)
    sign_bit = jnp.uint32(0x80000000)

    acc_ref[...] = jnp.zeros_like(acc_ref)

    def blocks(js, carries, n_diag=0):
        ks = [k_ref[0, pl.ds(pl.multiple_of(j * tq, tq), tq), :] for j in js]
        vts = [vt_ref[0, :, pl.ds(pl.multiple_of(j * tq, tq), tq)] for j in js]
        chains = [(b, h) for b in range(len(js)) for h in heads]
        carry = list(carries)
        acc = [acc_ref[h] for h in heads]
        z, lbeta, sp, sp0, excl, w = {}, {}, {}, {}, {}, {}

        def stage(c, s):
            b, h = c
            if s == 0:
                q_h, t = qh[h]
                z[c] = lax.dot_general(ks[b][:, t * LANES:(t + 1) * LANES], q_h, NT_DIMS,
                                       preferred_element_type=F32)
            elif s == 1:
                zc = z.pop(c)
                neg_abs = lax.bitcast_convert_type(lax.bitcast_convert_type(zc, jnp.uint32) | sign_bit, F32)
                v = jnp.maximum(zc, 0.0) + jnp.log2(1.0 + jnp.exp2(neg_abs))
                lbeta[c] = zc - v
                if b < n_diag:
                    v = jnp.where(past, v, 0.0)
                sp0[c] = v[0:1, :]
                sp[c] = v.astype(BF16)
            elif s == 2:
                excl[c] = jnp.dot(neg_upper, sp.pop(c), preferred_element_type=F32)
            elif s == 3:
                v = jnp.exp2((lbeta.pop(c) + excl[c] + carry[h]).astype(BF16))
                if b < n_diag:
                    v = jnp.where(past, v, jnp.zeros_like(v))
                w[c] = v
                carry[h] = carry[h] + excl.pop(c)[0:1, :] - sp0.pop(c)
            else:
                vt_h = vts[b][h * dh:(h + 1) * dh, :]
                acc[h] = acc[h] + jnp.dot(vt_h, w.pop(c), preferred_element_type=F32)

        n_stages = 5
        for t in range(len(chains) + n_stages - 1):
            for ci, c in enumerate(chains):
                if 0 <= t - ci < n_stages:
                    stage(c, t - ci)
        for h in heads:
            acc_ref[h] = acc[h]
        return tuple(carry)

    def alive(carries):
        return (jnp.max(functools.reduce(jnp.maximum, carries)) > DEAD_LOG2).astype(jnp.int32)

    def walk(n, tiles_of, state):
        def body(st):
            carries = blocks(tiles_of(st[0]), st[2:])
            return (st[0] + 1, alive(carries)) + carries
        st = lax.while_loop(lambda st: (st[0] < n) & (st[1] > 0), body, (jnp.int32(0),) + state)
        return st[1:]

    zeros = tuple(jnp.zeros((1, tq), F32) for _ in heads)
    carries = lax.cond(i > 0,
                       lambda: blocks([i, i - 1], zeros, n_diag=1),
                       lambda: blocks([i], zeros, n_diag=1))
    rest = jnp.maximum(i - 1, 0)
    n_single = jnp.where(rest > 0, rest - group_size * ((rest - 1) // group_size), 0)
    state = walk(n_single, lambda t: [i - 2 - t], (alive(carries),) + carries)
    top = i - 2 - n_single
    walk((rest - n_single) // group_size,
         lambda t: [top - group_size * t - g for g in range(group_size)], state)
    o_ref[0] = jnp.concatenate([acc_ref[h] for h in heads], axis=0).T.astype(o_ref.dtype)


def _sb_attention(qk, vt, d, tq=256, group_size=4, heads_per_step=4):
    b, s, _ = qk.shape
    dh = d // N_HEADS
    width = heads_per_step * dh
    assert LANES % dh == 0 and width % LANES == 0 and d % width == 0 and s % tq == 0
    nstep = d // width
    return pl.pallas_call(
        functools.partial(_sb_body, tq=tq, dh=dh, group_size=group_size),
        grid=(b, nstep, s // tq),
        in_specs=[pl.BlockSpec((1, tq, width), lambda bi, hp, i: (bi, i, hp)),
                  pl.BlockSpec((1, s, width), lambda bi, hp, i: (bi, 0, nstep + hp)),
                  pl.BlockSpec((1, width, s), lambda bi, hp, i: (bi, hp, 0))],
        out_specs=pl.BlockSpec((1, tq, width), lambda bi, hp, i: (bi, i, hp)),
        out_shape=jax.ShapeDtypeStruct((b, s, d), BF16),
        scratch_shapes=[pltpu.VMEM((heads_per_step, dh, tq), F32)],
        compiler_params=_params("parallel", "parallel", "arbitrary"),
        name="sb_attention",
    )(qk, qk, vt)


def _moba_body(slopes_ref, dead_ref, q_ref, k_ref, vt_ref, km_ref, o_ref,
               acc_ref, m_ref, bias_ref, off_ref, ua_ref, ub_ref, vta_ref, *, tq, dh, nb, topk, group_size):
    hp = pl.program_id(1)
    i = pl.program_id(2)

    @pl.when(i == 0)
    def _():
        row = lax.broadcasted_iota(jnp.int32, (BF16_SUBLANES, vta_ref.shape[2]), 0)
        tail = jnp.where(row == 0, 1.0, 0.0).astype(BF16)
        for h in range(vta_ref.shape[0]):
            vta_ref[h, :dh, :] = vt_ref[0, h * dh:(h + 1) * dh, :]
            vta_ref[h, dh:, :] = tail

    per_head = _per_head(q_ref[0], dh)
    qh = [q_h for q_h, _ in per_head]
    tile_of = [slice(t * LANES, (t + 1) * LANES) for _, t in per_head]
    heads = range(len(qh))
    km = km_ref[0]
    key = lax.broadcasted_iota(jnp.int32, (tq, tq), 0)
    qry = lax.broadcasted_iota(jnp.int32, (tq, tq), 1)
    rel = (qry - key).astype(F32)
    causal = qry >= key
    blk = lax.broadcasted_iota(jnp.int32, (nb, tq), 0).astype(F32)
    own = i.astype(F32)
    start_own = pl.multiple_of(i * tq, tq)
    k_own = k_ref[0, pl.ds(start_own, tq), :]
    slope = [slopes_ref[len(qh) * hp + h] for h in heads]

    n_dead = dead_ref[(pl.program_id(0) * pl.num_programs(1) + hp) * nb + i]

    def tiles_of(g):
        return [jnp.where(g >= 0, jnp.minimum(group_size * g + t, nb - 1), nb - 1) for t in range(group_size)]

    def logits(g, u_ref):
        for t, j in enumerate(tiles_of(g)):
            k2 = k_ref[0, pl.ds(pl.multiple_of(j * tq, tq), tq), :]
            for h in heads:
                u_ref[h, t] = lax.dot_general(k2[:, tile_of[h]], qh[h], NT_DIMS,
                                              preferred_element_type=F32) + bias_ref[h]

    for h in heads:
        bias_ref[h] = -slope[h] * rel
    top = (i + group_size - 1) // group_size - 1
    logits(top, ua_ref)

    km_hi = km.astype(BF16)
    km_mid = (km - km_hi.astype(F32)).astype(BF16)
    km_lo = (km - km_hi.astype(F32) - km_mid.astype(F32)).astype(BF16)
    for h in heads:
        gate = sum(lax.dot_general(part[:, tile_of[h]], qh[h], NT_DIMS, preferred_element_type=F32)
                   for part in (km_lo, km_mid, km_hi))
        gate = jnp.where(blk < own, gate, -jnp.inf)
        sel = jnp.zeros((nb, tq), F32)
        for _ in range(topk):
            mx = jnp.max(gate, axis=0, keepdims=True)
            idx = jnp.min(jnp.where(gate == mx, blk, float(nb)), axis=0, keepdims=True)
            pick = blk == idx
            sel = jnp.where(pick & (blk < own), 1.0, sel)
            gate = jnp.where(pick, -jnp.inf, gate)
        off = jnp.where(sel > 0.0, -slope[h] * float(tq) * (own - blk), MASKED)
        for r in range(nb):
            off_ref[h, r] = off[r:r + 1, :]
        z = lax.dot_general(k_own[:, tile_of[h]], qh[h], NT_DIMS, preferred_element_type=F32)
        s = jnp.where(causal, z + bias_ref[h], MASKED)
        m = jnp.max(s, axis=0, keepdims=True)
        p = jnp.exp2((s - m).astype(BF16))
        m_ref[h] = m
        acc_ref[h] = jnp.dot(vta_ref[h, :, pl.ds(start_own, tq)], p, preferred_element_type=F32)

    def update(g, u_ref):
        js = tiles_of(g)
        offs = [[off_ref[h, j] for j in js] for h in heads]
        m_new, acc = [], []
        for h in heads:
            m_old = m_ref[h]
            mh = m_old
            for t in range(group_size):
                mh = jnp.maximum(mh, jnp.max(u_ref[h, t], axis=0, keepdims=True) + offs[h][t])
            m_new.append(mh)
            acc.append(jnp.exp2(m_old - mh) * acc_ref[h])
        for t, j in enumerate(js):
            start = pl.multiple_of(j * tq, tq)
            for h in heads:
                p = jnp.exp2((u_ref[h, t] + (offs[h][t] - m_new[h])).astype(BF16))
                acc[h] = acc[h] + jnp.dot(vta_ref[h, :, pl.ds(start, tq)], p, preferred_element_type=F32)
        for h in heads:
            m_ref[h] = m_new[h]
            acc_ref[h] = acc[h]

    first = n_dead // group_size
    n_groups = top + 1 - first

    def two_groups(t, carry):
        g = top - 2 * t
        logits(g - 1, ub_ref)
        update(g, ua_ref)
        logits(g - 2, ua_ref)
        update(g - 1, ub_ref)
        return carry

    lax.fori_loop(0, (n_groups + 1) // 2, two_groups, 0)

    o_t = jnp.concatenate([acc_ref[h, :dh, :] / acc_ref[h, dh:dh + 1, :] for h in heads], axis=0)
    o_ref[0] = o_t.T.astype(o_ref.dtype)


def _block_norms(n2):
    b, nb = n2.shape[:2]
    return NORM_MARGIN * jnp.sqrt(n2.reshape(b, nb, LANES)[:, :, :N_HEADS])


def _dead_blocks(q_norm, k_norm, slopes, tq):
    nb = q_norm.shape[1]
    qn = q_norm.transpose(0, 2, 1)[:, :, :, None]
    kn = k_norm.transpose(0, 2, 1)
    kp = lax.cummax(kn, axis=2)[:, :, None, :]
    i = jnp.arange(nb, dtype=F32)[:, None]
    j = jnp.arange(nb, dtype=F32)[None, :]
    nearest = tq * (i - j) - (tq - 1)
    bound = qn * kp + qn * kn[:, :, :, None] - slopes[None, :, None, None] * nearest
    dead = (bound <= DEAD_LOG2) & (j < i)
    return jnp.sum(dead, axis=-1).astype(jnp.int32)


def _moba_attention(q, k, vt, kmeans, q_norm, k_norm, slopes, group_size=2, heads_per_step=2):
    b, s, d = q.shape
    dh = d // N_HEADS
    tq = MOBA_BLOCK
    width = heads_per_step * dh
    assert LANES % dh == 0 and width % LANES == 0 and d % width == 0 and s % tq == 0
    nb = s // tq
    nstep = d // width
    nh = heads_per_step
    n_dead = _dead_blocks(q_norm, k_norm, slopes, tq).reshape(b, nstep, nh, nb).min(axis=2).reshape(-1)
    return pl.pallas_call(
        functools.partial(_moba_body, tq=tq, dh=dh, nb=nb, topk=min(MOBA_TOPK, nb), group_size=group_size),
        grid=(b, nstep, nb),
        in_specs=[pl.BlockSpec(memory_space=pltpu.SMEM),
                  pl.BlockSpec(memory_space=pltpu.SMEM),
                  pl.BlockSpec((1, tq, width), lambda bi, hp, i: (bi, i, hp)),
                  pl.BlockSpec((1, s, width), lambda bi, hp, i: (bi, 0, hp)),
                  pl.BlockSpec((1, width, s), lambda bi, hp, i: (bi, hp, 0)),
                  pl.BlockSpec((1, nb, width), lambda bi, hp, i: (bi, 0, hp))],
        out_specs=pl.BlockSpec((1, tq, width), lambda bi, hp, i: (bi, i, hp)),
        out_shape=jax.ShapeDtypeStruct((b, s, d), BF16),
        scratch_shapes=[pltpu.VMEM((nh, dh + BF16_SUBLANES, tq), F32),
                        pltpu.VMEM((nh, 1, tq), F32),
                        pltpu.VMEM((nh, tq, tq), F32),
                        pltpu.VMEM((nh, nb, 1, tq), F32),
                        pltpu.VMEM((nh, group_size, tq, tq), F32),
                        pltpu.VMEM((nh, group_size, tq, tq), F32),
                        pltpu.VMEM((nh, dh + BF16_SUBLANES, s), BF16)],
        compiler_params=_params("arbitrary", "arbitrary", "arbitrary"),
        name="moba_attention",
    )(slopes, n_dead, q, k, vt, kmeans)


def _proj_res_body(o_ref, w_ref, x_ref, gt_ref, out_ref):
    y = jnp.dot(o_ref[0], w_ref[...], preferred_element_type=F32)
    out_ref[0] = x_ref[0] + (1.0 + gt_ref[0]) * y


def _proj_residual(o, w, x, mod, gate_idx, tm=512):
    b, s, d = x.shape
    return pl.pallas_call(
        _proj_res_body,
        grid=(b, s // tm),
        in_specs=[pl.BlockSpec((1, tm, d), lambda bi, i: (bi, i, 0)),
                  pl.BlockSpec((d, d), lambda bi, i: (0, 0)),
                  pl.BlockSpec((1, tm, d), lambda bi, i: (bi, i, 0)),
                  pl.BlockSpec((1, 1, d), lambda bi, i: (bi, 0, gate_idx))],
        out_specs=pl.BlockSpec((1, tm, d), lambda bi, i: (bi, i, 0)),
        out_shape=jax.ShapeDtypeStruct((b, s, d), F32),
        compiler_params=_params("parallel", "parallel"),
        name="proj_residual",
    )(o, w, x, mod)


def _route(logits):
    tm = logits.shape[0]
    lane = lax.broadcasted_iota(jnp.int32, (tm, LANES), 1).astype(F32)
    big = float(LANES)
    is_grp = (lane >= float(N_EXPERTS)) & (lane < float(N_EXPERTS + N_GROUPS))
    lg = jnp.where(is_grp, logits, -jnp.inf)
    mg = jnp.max(lg, axis=-1, keepdims=True)
    p_g = 1.0 / jnp.sum(jnp.exp(lg - mg), axis=-1, keepdims=True)
    g_sel = jnp.min(jnp.where(lg == mg, lane, big), axis=-1, keepdims=True) - float(N_EXPERTS)
    lo = g_sel * float(EXPERTS_PER_GROUP)
    in_grp = (lane >= lo) & (lane < lo + float(EXPERTS_PER_GROUP))
    le = jnp.where(in_grp, logits, -jnp.inf)
    m1 = jnp.max(le, axis=-1, keepdims=True)
    i1 = jnp.min(jnp.where(le == m1, lane, big), axis=-1, keepdims=True)
    le2 = jnp.where(lane == i1, -jnp.inf, le)
    m2 = jnp.max(le2, axis=-1, keepdims=True)
    i2 = jnp.min(jnp.where(le2 == m2, lane, big), axis=-1, keepdims=True)
    e2 = jnp.exp(m2 - m1)
    den = 1.0 + e2
    return jnp.where(lane == i1, p_g / den, 0.0) + jnp.where(lane == i2, p_g * e2 / den, 0.0)


def _moe_body(x_ref, g_ref, sh_ref, sc_ref, gt_ref, wr_ref, br_ref, wg_ref, wu_ref, wd_ref, gf_ref,
              out_ref, h_ref, comb_ref, acc_ref, *, final_norm):
    gi = pl.program_id(1)

    @pl.when(gi == 0)
    def _():
        h = _norm_mod(x_ref[...], g_ref[...], sh_ref[0], sc_ref[0])
        h_hi = h.astype(BF16)
        h_ref[...] = h_hi
        h_lo = (h - h_hi.astype(F32)).astype(BF16)
        w = wr_ref[...]
        w_hi = w.astype(BF16)
        w_lo = (w - w_hi.astype(F32)).astype(BF16)
        logits = (jnp.dot(h_hi, w_hi, preferred_element_type=F32)
                  + (jnp.dot(h_hi, w_lo, preferred_element_type=F32)
                     + jnp.dot(h_lo, w_hi, preferred_element_type=F32))) + br_ref[...]
        comb_ref[...] = _route(logits)
        acc_ref[...] = jnp.zeros_like(acc_ref)

    h = h_ref[...]
    comb = comb_ref[...]
    lane = lax.broadcasted_iota(jnp.int32, comb.shape, 1)
    for e in range(EXPERTS_PER_GROUP):
        ce = jnp.sum(jnp.where(lane == gi * EXPERTS_PER_GROUP + e, comb, 0.0), axis=-1, keepdims=True)
        hg = jnp.dot(h, wg_ref[e], preferred_element_type=F32)
        hu = jnp.dot(h, wu_ref[e], preferred_element_type=F32)
        hid = hg * (1.0 / (1.0 + jnp.exp(-hg))) * hu * ce
        acc_ref[...] += jnp.dot(hid.astype(BF16), wd_ref[e], preferred_element_type=F32)

    @pl.when(gi == N_GROUPS - 1)
    def _():
        y = x_ref[...] + (1.0 + gt_ref[0]) * acc_ref[...]
        if final_norm:
            ms = jnp.mean(y * y, axis=-1, keepdims=True)
            y = y * lax.rsqrt(ms + RMS_EPS) * gf_ref[...]
        out_ref[...] = y


def _moe(x, g, mod, shift_idx, scale_idx, gate_idx, w_router, b_router, w_gate, w_up, w_down, g_final,
         final_norm, tm=1024):
    b, s, d = x.shape
    ff = w_gate.shape[2]
    t = b * s
    per_batch = s // tm
    epg = EXPERTS_PER_GROUP
    out = pl.pallas_call(
        functools.partial(_moe_body, final_norm=final_norm),
        grid=(t // tm, N_GROUPS),
        in_specs=[pl.BlockSpec((tm, d), lambda i, gi: (i, 0)),
                  pl.BlockSpec((1, d), lambda i, gi: (0, 0)),
                  pl.BlockSpec((1, 1, d), lambda i, gi: (i // per_batch, 0, shift_idx)),
                  pl.BlockSpec((1, 1, d), lambda i, gi: (i // per_batch, 0, scale_idx)),
                  pl.BlockSpec((1, 1, d), lambda i, gi: (i // per_batch, 0, gate_idx)),
                  pl.BlockSpec((d, LANES), lambda i, gi: (0, 0)),
                  pl.BlockSpec((1, LANES), lambda i, gi: (0, 0)),
                  pl.BlockSpec((epg, d, ff), lambda i, gi: (gi, 0, 0)),
                  pl.BlockSpec((epg, d, ff), lambda i, gi: (gi, 0, 0)),
                  pl.BlockSpec((epg, ff, d), lambda i, gi: (gi, 0, 0)),
                  pl.BlockSpec((1, d), lambda i, gi: (0, 0))],
        out_specs=pl.BlockSpec((tm, d), lambda i, gi: (i, 0)),
        out_shape=jax.ShapeDtypeStruct((t, d), F32),
        scratch_shapes=[pltpu.VMEM((tm, d), BF16),
                        pltpu.VMEM((tm, LANES), F32),
                        pltpu.VMEM((tm, d), F32)],
        compiler_params=_params("parallel", "arbitrary"),
        name="moe_final" if final_norm else "moe",
    )(x.reshape(t, d), g.reshape(1, d), mod, mod, mod, w_router, b_router, w_gate, w_up, w_down,
      g_final.reshape(1, d))
    return out.reshape(b, s, d)


def _router_params(w_rg, b_rg, w_re, b_re):
    d = w_rg.shape[0]
    pad = LANES - N_EXPERTS - N_GROUPS
    w = jnp.concatenate([w_re, w_rg, jnp.zeros((d, pad), F32)], axis=1)
    bias = jnp.concatenate([b_re, b_rg, jnp.zeros((pad,), F32)]).reshape(1, LANES)
    return w, bias


def kernel(x, c, w_ada, b_ada, g_attn, g_ffn, w_qkv_a, w_o_a, w_ada_kv, b_ada_kv, g_kv, w_kv, w_q_b, w_o_b,
           w_rg, b_rg, w_re, b_re, w_gate, w_up, w_down, g_final):
    b, s, d = x.shape
    depth = w_ada.shape[0]
    n_a = depth // 2
    q_scale = LOG2E * (d // N_HEADS) ** -0.5
    slopes = LOG2E * jnp.exp2(-ALIBI_MAX_BIAS * jnp.arange(1, N_HEADS + 1, dtype=F32) / N_HEADS)
    mods = _ada(c, w_ada, b_ada).reshape(depth, b, 1, 6 * d)
    mod_kv = _ada(c, w_ada_kv[None], b_ada_kv[None]).reshape(b, 1, 2 * d)
    k = vt = kmeans = k_norm = None
    for l in range(depth):
        mod = mods[l]
        if l < n_a:
            w_qkv = jnp.concatenate([w_qkv_a[l][:, :d] * q_scale, w_qkv_a[l][:, d:]], axis=1).astype(BF16)
            qk, v_t = _norm_matmul(x, g_attn[l], mod, 0, 1, w_qkv, with_t=True)
            o = _sb_attention(qk, v_t, d)
            w_o = w_o_a[l]
        else:
            j = l - n_a
            q, qn2 = _norm_matmul(x, g_attn[l], mod, 0, 1, (w_q_b[j] * q_scale).astype(BF16), with_norms=True)
            o = _moba_attention(q, k, vt, kmeans, _block_norms(qn2), k_norm, slopes)
            w_o = w_o_b[j]
        x = _proj_residual(o, w_o.astype(BF16), x, mod, 2)
        w_router, b_router = _router_params(w_rg[l], b_rg[l], w_re[l], b_re[l])
        x = _moe(x, g_ffn[l], mod, 3, 4, 5, w_router, b_router, w_gate[l].astype(BF16), w_up[l].astype(BF16),
                 w_down[l].astype(BF16), g_final, final_norm=(l == depth - 1))
        if l == n_a - 1:
            k, vt, kmeans, kn2 = _norm_matmul(x, g_kv, mod_kv, 0, 1, w_kv.astype(BF16), with_t=True,
                                              with_means=True, with_norms=True)
            kmeans = kmeans.reshape(b, s // MOBA_BLOCK, d)
            k_norm = _block_norms(kn2)
    return x
```

```python
import functools
import math

import jax
import jax.numpy as jnp
from jax import lax
from jax.experimental import pallas as pl
from jax.experimental.pallas import tpu as pltpu

N_HEADS = 16
MOBA_BLOCK = 256
MOBA_TOPK = 3
N_GROUPS = 4
EXPERTS_PER_GROUP = 4
N_EXPERTS = N_GROUPS * EXPERTS_PER_GROUP
RMS_EPS = 1e-6
ALIBI_MAX_BIAS = 8.0

LANES = 128
BF16_SUBLANES = 16
VMEM_LIMIT = 56 * 1024 * 1024
MASKED = -1e30
LOG2E = math.log2(math.e)
NORM_MARGIN = 1.02
DEAD_LOG2 = -160.0

F32 = jnp.float32
BF16 = jnp.bfloat16
NT_DIMS = (((1,), (1,)), ((), ()))


def _params(*sem):
    return pltpu.CompilerParams(dimension_semantics=sem, vmem_limit_bytes=VMEM_LIMIT)


def _ada_body(c_ref, w_ref, b_ref, o_ref):
    o_ref[0] = jnp.dot(c_ref[...], w_ref[0], preferred_element_type=F32,
                       precision=lax.Precision.HIGHEST) + b_ref[0]


def _ada(c, w, b, tn=512):
    nl, d, n = w.shape
    bsz = c.shape[0]
    return pl.pallas_call(
        _ada_body,
        grid=(nl, n // tn),
        in_specs=[pl.BlockSpec((bsz, d), lambda l, j: (0, 0)),
                  pl.BlockSpec((1, d, tn), lambda l, j: (l, 0, j)),
                  pl.BlockSpec((1, 1, tn), lambda l, j: (l, 0, j))],
        out_specs=pl.BlockSpec((1, bsz, tn), lambda l, j: (l, 0, j)),
        out_shape=jax.ShapeDtypeStruct((nl, bsz, n), F32),
        compiler_params=_params("parallel", "parallel"),
        name="ada_mod",
    )(c, w, b.reshape(nl, 1, n))


def _norm_mod(x, g, shift, scale):
    ms = jnp.mean(x * x, axis=-1, keepdims=True)
    y = x * lax.rsqrt(ms + RMS_EPS) * g
    return y * (1.0 + scale) + shift


def _norm_mm_body(x_ref, g_ref, sh_ref, sc_ref, w_ref, o_ref, *rest, n_plain, d_model, with_t, with_means,
                  with_norms):
    h = _norm_mod(x_ref[0], g_ref[...], sh_ref[0], sc_ref[0])
    y = jnp.dot(h.astype(BF16), w_ref[...], preferred_element_type=F32)
    o_ref[0] = y[:, :n_plain].astype(o_ref.dtype)
    if with_t:
        rest[0][0] = y[:, n_plain:].T.astype(BF16)
    n_blocks = y.shape[0] // MOBA_BLOCK
    if with_means:
        km_ref = rest[1 if with_t else 0]
        for r in range(n_blocks):
            km_ref[0, r] = jnp.mean(y[r * MOBA_BLOCK:(r + 1) * MOBA_BLOCK, :d_model], axis=0, keepdims=True)
    if with_norms:
        yb = y[:, :d_model].astype(BF16).astype(F32)
        dcol = lax.broadcasted_iota(jnp.int32, (d_model, LANES), 0)
        head = lax.broadcasted_iota(jnp.int32, (d_model, LANES), 1)
        seg = jnp.where(dcol // (d_model // N_HEADS) == head, 1.0, 0.0).astype(BF16)
        n2 = jnp.dot((yb * yb).astype(BF16), seg, preferred_element_type=F32)
        for r in range(n_blocks):
            rest[-1][0, r] = jnp.max(n2[r * MOBA_BLOCK:(r + 1) * MOBA_BLOCK], axis=0, keepdims=True)


def _norm_matmul(x, g, mod, shift_idx, scale_idx, w, with_t=False, with_means=False, with_norms=False, tm=512):
    b, s, d = x.shape
    n = w.shape[1]
    n_plain = n - d if with_t else n
    out_shape = [jax.ShapeDtypeStruct((b, s, n_plain), BF16)]
    out_specs = [pl.BlockSpec((1, tm, n_plain), lambda bi, i: (bi, i, 0))]
    if with_t:
        out_shape.append(jax.ShapeDtypeStruct((b, d, s), BF16))
        out_specs.append(pl.BlockSpec((1, d, tm), lambda bi, i: (bi, 0, i)))
    if with_means:
        out_shape.append(jax.ShapeDtypeStruct((b, s // MOBA_BLOCK, 1, d), F32))
        out_specs.append(pl.BlockSpec((1, tm // MOBA_BLOCK, 1, d), lambda bi, i: (bi, i, 0, 0)))
    if with_norms:
        out_shape.append(jax.ShapeDtypeStruct((b, s // MOBA_BLOCK, 1, LANES), F32))
        out_specs.append(pl.BlockSpec((1, tm // MOBA_BLOCK, 1, LANES), lambda bi, i: (bi, i, 0, 0)))
    return pl.pallas_call(
        functools.partial(_norm_mm_body, n_plain=n_plain, d_model=d, with_t=with_t, with_means=with_means,
                          with_norms=with_norms),
        grid=(b, s // tm),
        in_specs=[pl.BlockSpec((1, tm, d), lambda bi, i: (bi, i, 0)),
                  pl.BlockSpec((1, d), lambda bi, i: (0, 0)),
                  pl.BlockSpec((1, 1, d), lambda bi, i: (bi, 0, shift_idx)),
                  pl.BlockSpec((1, 1, d), lambda bi, i: (bi, 0, scale_idx)),
                  pl.BlockSpec((d, n), lambda bi, i: (0, 0))],
        out_specs=out_specs,
        out_shape=out_shape,
        compiler_params=_params("parallel", "parallel"),
        name="norm_matmul" + ("_t" if with_t else "") + ("_means" if with_means else "") + ("_norms" if with_norms else ""),
    )(x, g.reshape(1, d), mod, mod, w)


def _per_head(q2, dh):
    lane = lax.broadcasted_iota(jnp.int32, (q2.shape[0], LANES), 1)
    out = []
    for t in range(q2.shape[1] // LANES):
        tile = q2[:, t * LANES:(t + 1) * LANES]
        zero = jnp.zeros_like(tile)
        for half in range(LANES // dh):
            mine = (lane >= half * dh) & (lane < (half + 1) * dh)
            out.append((jnp.where(mine, tile, zero), t))
    return out


def _sb_body(q_ref, k_ref, vt_ref, o_ref, acc_ref, *, tq, dh, group_size):
    i = pl.program_id(2)
    qh = _per_head(q_ref[0], dh)
    heads = range(len(qh))
    key = lax.broadcasted_iota(jnp.int32, (tq, tq), 0)
    qry = lax.broadcasted_iota(jnp.int32, (tq, tq), 1)
    past = key < qry
    neg_upper = jnp.where(qry > key, -1.0, 0.0).astype(BF16)
    sign_bit = jnp.uint32(0x80000000)

    acc_ref[...] = jnp.zeros_like(acc_ref)

    def blocks(js, carries, n_diag=0):
        ks = [k_ref[0, pl.ds(pl.multiple_of(j * tq, tq), tq), :] for j in js]
        vts = [vt_ref[0, :, pl.ds(pl.multiple_of(j * tq, tq), tq)] for j in js]
        chains = [(b, h) for b in range(len(js)) for h in heads]
        carry = list(carries)
        acc = [acc_ref[h] for h in heads]
        z, lbeta, sp, sp0, excl, w = {}, {}, {}, {}, {}, {}

        def stage(c, s):
            b, h = c
            if s == 0:
                q_h, t = qh[h]
                z[c] = lax.dot_general(ks[b][:, t * LANES:(t + 1) * LANES], q_h, NT_DIMS,
                                       preferred_element_type=F32)
            elif s == 1:
                zc = z.pop(c)
                neg_abs = lax.bitcast_convert_type(lax.bitcast_convert_type(zc, jnp.uint32) | sign_bit, F32)
                v = jnp.maximum(zc, 0.0) + jnp.log2(1.0 + jnp.exp2(neg_abs))
                lbeta[c] = zc - v
                if b < n_diag:
                    v = jnp.where(past, v, 0.0)
                sp0[c] = v[0:1, :]
                sp[c] = v.astype(BF16)
            elif s == 2:
                excl[c] = jnp.dot(neg_upper, sp.pop(c), preferred_element_type=F32)
            elif s == 3:
                v = jnp.exp2((lbeta.pop(c) + excl[c] + carry[h]).astype(BF16))
                if b < n_diag:
                    v = jnp.where(past, v, jnp.zeros_like(v))
                w[c] = v
                carry[h] = carry[h] + excl.pop(c)[0:1, :] - sp0.pop(c)
            else:
                vt_h = vts[b][h * dh:(h + 1) * dh, :]
                acc[h] = acc[h] + jnp.dot(vt_h, w.pop(c), preferred_element_type=F32)

        n_stages = 5
        for t in range(len(chains) + n_stages - 1):
            for ci, c in enumerate(chains):
                if 0 <= t - ci < n_stages:
                    stage(c, t - ci)
        for h in heads:
            acc_ref[h] = acc[h]
        return tuple(carry)

    def alive(carries):
        return (jnp.max(functools.reduce(jnp.maximum, carries)) > DEAD_LOG2).astype(jnp.int32)

    def walk(n, tiles_of, state):
        def body(st):
            carries = blocks(tiles_of(st[0]), st[2:])
            return (st[0] + 1, alive(carries)) + carries
        st = lax.while_loop(lambda st: (st[0] < n) & (st[1] > 0), body, (jnp.int32(0),) + state)
        return st[1:]

    zeros = tuple(jnp.zeros((1, tq), F32) for _ in heads)
    carries = lax.cond(i > 0,
                       lambda: blocks([i, i - 1], zeros, n_diag=1),
                       lambda: blocks([i], zeros, n_diag=1))
    rest = jnp.maximum(i - 1, 0)
    n_single = jnp.where(rest > 0, rest - group_size * ((rest - 1) // group_size), 0)
    state = walk(n_single, lambda t: [i - 2 - t], (alive(carries),) + carries)
    top = i - 2 - n_single
    walk((rest - n_single) // group_size,
         lambda t: [top - group_size * t - g for g in range(group_size)], state)
    o_ref[0] = jnp.concatenate([acc_ref[h] for h in heads], axis=0).T.astype(o_ref.dtype)


def _sb_attention(qk, vt, d, tq=256, group_size=4, heads_per_step=4):
    b, s, _ = qk.shape
    dh = d // N_HEADS
    width = heads_per_step * dh
    assert LANES % dh == 0 and width % LANES == 0 and d % width == 0 and s % tq == 0
    nstep = d // width
    return pl.pallas_call(
        functools.partial(_sb_body, tq=tq, dh=dh, group_size=group_size),
        grid=(b, nstep, s // tq),
        in_specs=[pl.BlockSpec((1, tq, width), lambda bi, hp, i: (bi, i, hp)),
                  pl.BlockSpec((1, s, width), lambda bi, hp, i: (bi, 0, nstep + hp)),
                  pl.BlockSpec((1, width, s), lambda bi, hp, i: (bi, hp, 0))],
        out_specs=pl.BlockSpec((1, tq, width), lambda bi, hp, i: (bi, i, hp)),
        out_shape=jax.ShapeDtypeStruct((b, s, d), BF16),
        scratch_shapes=[pltpu.VMEM((heads_per_step, dh, tq), F32)],
        compiler_params=_params("parallel", "parallel", "arbitrary"),
        name="sb_attention",
    )(qk, qk, vt)


def _moba_body(slopes_ref, dead_ref, q_ref, k_ref, vt_ref, km_ref, o_ref,
               acc_ref, m_ref, bias_ref, off_ref, ua_ref, ub_ref, cma_ref, cmb_ref, vta_ref,
               *, tq, dh, nb, topk, group_size):
    hp = pl.program_id(1)
    i = pl.program_id(2)

    @pl.when(i == 0)
    def _():
        row = lax.broadcasted_iota(jnp.int32, (BF16_SUBLANES, vta_ref.shape[2]), 0)
        tail = jnp.where(row == 0, 1.0, 0.0).astype(BF16)
        for h in range(vta_ref.shape[0]):
            vta_ref[h, :dh, :] = vt_ref[0, h * dh:(h + 1) * dh, :]
            vta_ref[h, dh:, :] = tail

    per_head = _per_head(q_ref[0], dh)
    qh = [q_h for q_h, _ in per_head]
    tile_of = [slice(t * LANES, (t + 1) * LANES) for _, t in per_head]
    heads = range(len(qh))
    km = km_ref[0]
    key = lax.broadcasted_iota(jnp.int32, (tq, tq), 0)
    qry = lax.broadcasted_iota(jnp.int32, (tq, tq), 1)
    rel = (qry - key).astype(F32)
    causal = qry >= key
    blk = lax.broadcasted_iota(jnp.int32, (nb, tq), 0).astype(F32)
    own = i.astype(F32)
    slope = [slopes_ref[len(qh) * hp + h] for h in heads]
    n_dead = dead_ref[(pl.program_id(0) * pl.num_programs(1) + hp) * nb + i]
    masked_row = jnp.full((1, tq), MASKED, F32)

    def tiles_of(g):
        return [i - group_size * g - t for t in range(group_size)]

    def logits(g, u_ref, cm_ref, with_own=False):
        for t, j in enumerate(tiles_of(g)):
            k2 = k_ref[0, pl.ds(pl.multiple_of(jnp.maximum(j, 0) * tq, tq), tq), :]
            for h in heads:
                u = lax.dot_general(k2[:, tile_of[h]], qh[h], NT_DIMS, preferred_element_type=F32) + bias_ref[h]
                if with_own and t == 0:
                    u = jnp.where(causal, u, MASKED)
                u_ref[h, t] = u
                cm_ref[h, t] = jnp.max(u, axis=0, keepdims=True)

    for h in heads:
        bias_ref[h] = -slope[h] * rel
        m_ref[h] = masked_row
        acc_ref[h] = jnp.zeros(acc_ref.shape[1:], F32)
    logits(0, ua_ref, cma_ref, with_own=True)

    km_hi = km.astype(BF16)
    km_mid = (km - km_hi.astype(F32)).astype(BF16)
    km_lo = (km - km_hi.astype(F32) - km_mid.astype(F32)).astype(BF16)
    for h in heads:
        gate = sum(lax.dot_general(part[:, tile_of[h]], qh[h], NT_DIMS, preferred_element_type=F32)
                   for part in (km_lo, km_mid, km_hi))
        gate = jnp.where(blk < own, gate, -jnp.inf)
        sel = jnp.zeros((nb, tq), F32)
        for _ in range(topk):
            mx = jnp.max(gate, axis=0, keepdims=True)
            idx = jnp.min(jnp.where(gate == mx, blk, float(nb)), axis=0, keepdims=True)
            pick = blk == idx
            sel = jnp.where(pick & (blk < own), 1.0, sel)
            gate = jnp.where(pick, -jnp.inf, gate)
        off = jnp.where((sel > 0.0) | (blk == own), -slope[h] * float(tq) * (own - blk), MASKED)
        for r in range(nb):
            off_ref[h, r] = off[r:r + 1, :]
        off_ref[h, nb] = masked_row

    def update(g, u_ref, cm_ref):
        js = tiles_of(g)
        offs = [[off_ref[h, jnp.where(j >= 0, j, nb)] for j in js] for h in heads]
        m_new, acc = [], []
        for h in heads:
            m_old = m_ref[h]
            mh = m_old
            for t in range(group_size):
                mh = jnp.maximum(mh, cm_ref[h, t] + offs[h][t])
            m_new.append(mh)
            acc.append(jnp.exp2(m_old - mh) * acc_ref[h])
        for t, j in enumerate(js):
            start = pl.multiple_of(jnp.maximum(j, 0) * tq, tq)
            for h in heads:
                p = jnp.exp2((u_ref[h, t] + (offs[h][t] - m_new[h])).astype(BF16))
                acc[h] = acc[h] + jnp.dot(vta_ref[h, :, pl.ds(start, tq)], p, preferred_element_type=F32)
        for h in heads:
            m_ref[h] = m_new[h]
            acc_ref[h] = acc[h]

    n_groups = (i - n_dead + group_size) // group_size

    def two_groups(t, carry):
        g = 2 * t
        logits(g + 1, ub_ref, cmb_ref)
        update(g, ua_ref, cma_ref)
        logits(g + 2, ua_ref, cma_ref)
        update(g + 1, ub_ref, cmb_ref)
        return carry

    lax.fori_loop(0, (n_groups + 1) // 2, two_groups, 0)

    o_t = jnp.concatenate([acc_ref[h, :dh, :] / acc_ref[h, dh:dh + 1, :] for h in heads], axis=0)
    o_ref[0] = o_t.T.astype(o_ref.dtype)


def _block_norms(n2):
    b, nb = n2.shape[:2]
    return NORM_MARGIN * jnp.sqrt(n2.reshape(b, nb, LANES)[:, :, :N_HEADS])


def _dead_blocks(q_norm, k_norm, slopes, tq):
    nb = q_norm.shape[1]
    qn = q_norm.transpose(0, 2, 1)[:, :, :, None]
    kn = k_norm.transpose(0, 2, 1)
    kp = lax.cummax(kn, axis=2)[:, :, None, :]
    i = jnp.arange(nb, dtype=F32)[:, None]
    j = jnp.arange(nb, dtype=F32)[None, :]
    nearest = tq * (i - j) - (tq - 1)
    bound = qn * kp + qn * kn[:, :, :, None] - slopes[None, :, None, None] * nearest
    dead = (bound <= DEAD_LOG2) & (j < i)
    return jnp.sum(dead, axis=-1).astype(jnp.int32)


def _moba_attention(q, k, vt, kmeans, q_norm, k_norm, slopes, group_size=2, heads_per_step=2):
    b, s, d = q.shape
    dh = d // N_HEADS
    tq = MOBA_BLOCK
    width = heads_per_step * dh
    assert LANES % dh == 0 and width % LANES == 0 and d % width == 0 and s % tq == 0
    nb = s // tq
    nstep = d // width
    nh = heads_per_step
    n_dead = _dead_blocks(q_norm, k_norm, slopes, tq).reshape(b, nstep, nh, nb).min(axis=2).reshape(-1)
    return pl.pallas_call(
        functools.partial(_moba_body, tq=tq, dh=dh, nb=nb, topk=min(MOBA_TOPK, nb), group_size=group_size),
        grid=(b, nstep, nb),
        in_specs=[pl.BlockSpec(memory_space=pltpu.SMEM),
                  pl.BlockSpec(memory_space=pltpu.SMEM),
                  pl.BlockSpec((1, tq, width), lambda bi, hp, i: (bi, i, hp)),
                  pl.BlockSpec((1, s, width), lambda bi, hp, i: (bi, 0, hp)),
                  pl.BlockSpec((1, width, s), lambda bi, hp, i: (bi, hp, 0)),
                  pl.BlockSpec((1, nb, width), lambda bi, hp, i: (bi, 0, hp))],
        out_specs=pl.BlockSpec((1, tq, width), lambda bi, hp, i: (bi, i, hp)),
        out_shape=jax.ShapeDtypeStruct((b, s, d), BF16),
        scratch_shapes=[pltpu.VMEM((nh, dh + BF16_SUBLANES, tq), F32),
                        pltpu.VMEM((nh, 1, tq), F32),
                        pltpu.VMEM((nh, tq, tq), F32),
                        pltpu.VMEM((nh, nb + 1, 1, tq), F32),
                        pltpu.VMEM((nh, group_size, tq, tq), F32),
                        pltpu.VMEM((nh, group_size, tq, tq), F32),
                        pltpu.VMEM((nh, group_size, 1, tq), F32),
                        pltpu.VMEM((nh, group_size, 1, tq), F32),
                        pltpu.VMEM((nh, dh + BF16_SUBLANES, s), BF16)],
        compiler_params=_params("arbitrary", "arbitrary", "arbitrary"),
        name="moba_attention",
    )(slopes, n_dead, q, k, vt, kmeans)


def _proj_res_body(o_ref, w_ref, x_ref, gt_ref, out_ref):
    y = jnp.dot(o_ref[0], w_ref[...], preferred_element_type=F32)
    out_ref[0] = x_ref[0] + (1.0 + gt_ref[0]) * y


def _proj_residual(o, w, x, mod, gate_idx, tm=512):
    b, s, d = x.shape
    return pl.pallas_call(
        _proj_res_body,
        grid=(b, s // tm),
        in_specs=[pl.BlockSpec((1, tm, d), lambda bi, i: (bi, i, 0)),
                  pl.BlockSpec((d, d), lambda bi, i: (0, 0)),
                  pl.BlockSpec((1, tm, d), lambda bi, i: (bi, i, 0)),
                  pl.BlockSpec((1, 1, d), lambda bi, i: (bi, 0, gate_idx))],
        out_specs=pl.BlockSpec((1, tm, d), lambda bi, i: (bi, i, 0)),
        out_shape=jax.ShapeDtypeStruct((b, s, d), F32),
        compiler_params=_params("parallel", "parallel"),
        name="proj_residual",
    )(o, w, x, mod)


def _route(logits):
    tm = logits.shape[0]
    lane = lax.broadcasted_iota(jnp.int32, (tm, LANES), 1).astype(F32)
    big = float(LANES)
    is_grp = (lane >= float(N_EXPERTS)) & (lane < float(N_EXPERTS + N_GROUPS))
    lg = jnp.where(is_grp, logits, -jnp.inf)
    mg = jnp.max(lg, axis=-1, keepdims=True)
    p_g = 1.0 / jnp.sum(jnp.exp(lg - mg), axis=-1, keepdims=True)
    g_sel = jnp.min(jnp.where(lg == mg, lane, big), axis=-1, keepdims=True) - float(N_EXPERTS)
    lo = g_sel * float(EXPERTS_PER_GROUP)
    in_grp = (lane >= lo) & (lane < lo + float(EXPERTS_PER_GROUP))
    le = jnp.where(in_grp, logits, -jnp.inf)
    m1 = jnp.max(le, axis=-1, keepdims=True)
    i1 = jnp.min(jnp.where(le == m1, lane, big), axis=-1, keepdims=True)
    le2 = jnp.where(lane == i1, -jnp.inf, le)
    m2 = jnp.max(le2, axis=-1, keepdims=True)
    i2 = jnp.min(jnp.where(le2 == m2, lane, big), axis=-1, keepdims=True)
    e2 = jnp.exp(m2 - m1)
    den = 1.0 + e2
    return jnp.where(lane == i1, p_g / den, 0.0) + jnp.where(lane == i2, p_g * e2 / den, 0.0)


def _moe_body(x_ref, g_ref, sh_ref, sc_ref, gt_ref, wr_ref, br_ref, wg_ref, wu_ref, wd_ref, gf_ref,
              out_ref, h_ref, comb_ref, acc_ref, *, final_norm):
    gi = pl.program_id(1)

    @pl.when(gi == 0)
    def _():
        h = _norm_mod(x_ref[...], g_ref[...], sh_ref[0], sc_ref[0])
        h_hi = h.astype(BF16)
        h_ref[...] = h_hi
        h_lo = (h - h_hi.astype(F32)).astype(BF16)
        w = wr_ref[...]
        w_hi = w.astype(BF16)
        w_lo = (w - w_hi.astype(F32)).astype(BF16)
        logits = (jnp.dot(h_hi, w_hi, preferred_element_type=F32)
                  + (jnp.dot(h_hi, w_lo, preferred_element_type=F32)
                     + jnp.dot(h_lo, w_hi, preferred_element_type=F32))) + br_ref[...]
        comb_ref[...] = _route(logits)
        acc_ref[...] = jnp.zeros_like(acc_ref)

    h = h_ref[...]
    comb = comb_ref[...]
    lane = lax.broadcasted_iota(jnp.int32, comb.shape, 1)
    for e in range(EXPERTS_PER_GROUP):
        ce = jnp.sum(jnp.where(lane == gi * EXPERTS_PER_GROUP + e, comb, 0.0), axis=-1, keepdims=True)
        hg = jnp.dot(h, wg_ref[e], preferred_element_type=F32)
        hu = jnp.dot(h, wu_ref[e], preferred_element_type=F32)
        hid = hg * (1.0 / (1.0 + jnp.exp(-hg))) * hu * ce
        acc_ref[...] += jnp.dot(hid.astype(BF16), wd_ref[e], preferred_element_type=F32)

    @pl.when(gi == N_GROUPS - 1)
    def _():
        y = x_ref[...] + (1.0 + gt_ref[0]) * acc_ref[...]
        if final_norm:
            ms = jnp.mean(y * y, axis=-1, keepdims=True)
            y = y * lax.rsqrt(ms + RMS_EPS) * gf_ref[...]
        out_ref[...] = y


def _moe(x, g, mod, shift_idx, scale_idx, gate_idx, w_router, b_router, w_gate, w_up, w_down, g_final,
         final_norm, tm=1024):
    b, s, d = x.shape
    ff = w_gate.shape[2]
    t = b * s
    per_batch = s // tm
    epg = EXPERTS_PER_GROUP
    out = pl.pallas_call(
        functools.partial(_moe_body, final_norm=final_norm),
        grid=(t // tm, N_GROUPS),
        in_specs=[pl.BlockSpec((tm, d), lambda i, gi: (i, 0)),
                  pl.BlockSpec((1, d), lambda i, gi: (0, 0)),
                  pl.BlockSpec((1, 1, d), lambda i, gi: (i // per_batch, 0, shift_idx)),
                  pl.BlockSpec((1, 1, d), lambda i, gi: (i // per_batch, 0, scale_idx)),
                  pl.BlockSpec((1, 1, d), lambda i, gi: (i // per_batch, 0, gate_idx)),
                  pl.BlockSpec((d, LANES), lambda i, gi: (0, 0)),
                  pl.BlockSpec((1, LANES), lambda i, gi: (0, 0)),
                  pl.BlockSpec((epg, d, ff), lambda i, gi: (gi, 0, 0)),
                  pl.BlockSpec((epg, d, ff), lambda i, gi: (gi, 0, 0)),
                  pl.BlockSpec((epg, ff, d), lambda i, gi: (gi, 0, 0)),
                  pl.BlockSpec((1, d), lambda i, gi: (0, 0))],
        out_specs=pl.BlockSpec((tm, d), lambda i, gi: (i, 0)),
        out_shape=jax.ShapeDtypeStruct((t, d), F32),
        scratch_shapes=[pltpu.VMEM((tm, d), BF16),
                        pltpu.VMEM((tm, LANES), F32),
                        pltpu.VMEM((tm, d), F32)],
        compiler_params=_params("parallel", "arbitrary"),
        name="moe_final" if final_norm else "moe",
    )(x.reshape(t, d), g.reshape(1, d), mod, mod, mod, w_router, b_router, w_gate, w_up, w_down,
      g_final.reshape(1, d))
    return out.reshape(b, s, d)


def _router_params(w_rg, b_rg, w_re, b_re):
    d = w_rg.shape[0]
    pad = LANES - N_EXPERTS - N_GROUPS
    w = jnp.concatenate([w_re, w_rg, jnp.zeros((d, pad), F32)], axis=1)
    bias = jnp.concatenate([b_re, b_rg, jnp.zeros((pad,), F32)]).reshape(1, LANES)
    return w, bias


def kernel(x, c, w_ada, b_ada, g_attn, g_ffn, w_qkv_a, w_o_a, w_ada_kv, b_ada_kv, g_kv, w_kv, w_q_b, w_o_b,
           w_rg, b_rg, w_re, b_re, w_gate, w_up, w_down, g_final):
    b, s, d = x.shape
    depth = w_ada.shape[0]
    n_a = depth // 2
    q_scale = LOG2E * (d // N_HEADS) ** -0.5
    slopes = LOG2E * jnp.exp2(-ALIBI_MAX_BIAS * jnp.arange(1, N_HEADS + 1, dtype=F32) / N_HEADS)
    mods = _ada(c, w_ada, b_ada).reshape(depth, b, 1, 6 * d)
    mod_kv = _ada(c, w_ada_kv[None], b_ada_kv[None]).reshape(b, 1, 2 * d)
    k = vt = kmeans = k_norm = None
    for l in range(depth):
        mod = mods[l]
        if l < n_a:
            w_qkv = jnp.concatenate([w_qkv_a[l][:, :d] * q_scale, w_qkv_a[l][:, d:]], axis=1).astype(BF16)
            qk, v_t = _norm_matmul(x, g_attn[l], mod, 0, 1, w_qkv, with_t=True)
            o = _sb_attention(qk, v_t, d)
            w_o = w_o_a[l]
        else:
            j = l - n_a
            q, qn2 = _norm_matmul(x, g_attn[l], mod, 0, 1, (w_q_b[j] * q_scale).astype(BF16), with_norms=True)
            o = _moba_attention(q, k, vt, kmeans, _block_norms(qn2), k_norm, slopes)
            w_o = w_o_b[j]
        x = _proj_residual(o, w_o.astype(BF16), x, mod, 2)
        w_router, b_router = _router_params(w_rg[l], b_rg[l], w_re[l], b_re[l])
        x = _moe(x, g_ffn[l], mod, 3, 4, 5, w_router, b_router, w_gate[l].astype(BF16), w_up[l].astype(BF16),
                 w_down[l].astype(BF16), g_final, final_norm=(l == depth - 1))
        if l == n_a - 1:
            k, vt, kmeans, kn2 = _norm_matmul(x, g_kv, mod_kv, 0, 1, w_kv.astype(BF16), with_t=True,
                                              with_means=True, with_norms=True)
            kmeans = kmeans.reshape(b, s // MOBA_BLOCK, d)
            k_norm = _block_norms(kn2)
    return x
```

```python
import functools
import math

import jax
import jax.numpy as jnp
from jax import lax
from jax.experimental import pallas as pl
from jax.experimental.pallas import tpu as pltpu

N_HEADS = 16
MOBA_BLOCK = 256
MOBA_TOPK = 3
N_GROUPS = 4
EXPERTS_PER_GROUP = 4
N_EXPERTS = N_GROUPS * EXPERTS_PER_GROUP
RMS_EPS = 1e-6
ALIBI_MAX_BIAS = 8.0

LANES = 128
BF16_SUBLANES = 16
VMEM_LIMIT = 56 * 1024 * 1024
MASKED = -1e30
LOG2E = math.log2(math.e)
NORM_MARGIN = 1.02
DEAD_LOG2 = -160.0

F32 = jnp.float32
BF16 = jnp.bfloat16


def _params(*sem):
    return pltpu.CompilerParams(dimension_semantics=sem, vmem_limit_bytes=VMEM_LIMIT)


def _ada_body(c_ref, w_ref, b_ref, o_ref):
    o_ref[0] = jnp.dot(c_ref[...], w_ref[0], preferred_element_type=F32,
                       precision=lax.Precision.HIGHEST) + b_ref[0]


def _ada(c, w, b, tn=512):
    nl, d, n = w.shape
    bsz = c.shape[0]
    return pl.pallas_call(
        _ada_body,
        grid=(nl, n // tn),
        in_specs=[pl.BlockSpec((bsz, d), lambda l, j: (0, 0)),
                  pl.BlockSpec((1, d, tn), lambda l, j: (l, 0, j)),
                  pl.BlockSpec((1, 1, tn), lambda l, j: (l, 0, j))],
        out_specs=pl.BlockSpec((1, bsz, tn), lambda l, j: (l, 0, j)),
        out_shape=jax.ShapeDtypeStruct((nl, bsz, n), F32),
        compiler_params=_params("parallel", "parallel"),
        name="ada_mod",
    )(c, w, b.reshape(nl, 1, n))


def _norm_mod(x, g, shift, scale):
    ms = jnp.mean(x * x, axis=-1, keepdims=True)
    y = x * lax.rsqrt(ms + RMS_EPS) * g
    return y * (1.0 + scale) + shift


def _norm_mm_body(x_ref, g_ref, sh_ref, sc_ref, w_ref, o_ref, *rest, n_plain, d_model, with_t, with_means,
                  with_norms):
    h = _norm_mod(x_ref[0], g_ref[...], sh_ref[0], sc_ref[0])
    y = jnp.dot(h.astype(BF16), w_ref[...], preferred_element_type=F32)
    o_ref[0] = y[:, :n_plain].astype(o_ref.dtype)
    if with_t:
        rest[0][0] = y[:, n_plain:].T.astype(BF16)
    n_blocks = y.shape[0] // MOBA_BLOCK
    if with_means:
        km_ref = rest[1 if with_t else 0]
        for r in range(n_blocks):
            km_ref[0, r] = jnp.mean(y[r * MOBA_BLOCK:(r + 1) * MOBA_BLOCK, :d_model], axis=0, keepdims=True)
    if with_norms:
        yb = y[:, :d_model].astype(BF16).astype(F32)
        dcol = lax.broadcasted_iota(jnp.int32, (d_model, LANES), 0)
        head = lax.broadcasted_iota(jnp.int32, (d_model, LANES), 1)
        seg = jnp.where(dcol // (d_model // N_HEADS) == head, 1.0, 0.0).astype(BF16)
        n2 = jnp.dot((yb * yb).astype(BF16), seg, preferred_element_type=F32)
        for r in range(n_blocks):
            rest[-1][0, r] = jnp.max(n2[r * MOBA_BLOCK:(r + 1) * MOBA_BLOCK], axis=0, keepdims=True)


def _norm_matmul(x, g, mod, shift_idx, scale_idx, w, with_t=False, with_means=False, with_norms=False, tm=512):
    b, s, d = x.shape
    n = w.shape[1]
    n_plain = n - d if with_t else n
    out_shape = [jax.ShapeDtypeStruct((b, s, n_plain), BF16)]
    out_specs = [pl.BlockSpec((1, tm, n_plain), lambda bi, i: (bi, i, 0))]
    if with_t:
        out_shape.append(jax.ShapeDtypeStruct((b, d, s), BF16))
        out_specs.append(pl.BlockSpec((1, d, tm), lambda bi, i: (bi, 0, i)))
    if with_means:
        out_shape.append(jax.ShapeDtypeStruct((b, s // MOBA_BLOCK, 1, d), F32))
        out_specs.append(pl.BlockSpec((1, tm // MOBA_BLOCK, 1, d), lambda bi, i: (bi, i, 0, 0)))
    if with_norms:
        out_shape.append(jax.ShapeDtypeStruct((b, s // MOBA_BLOCK, 1, LANES), F32))
        out_specs.append(pl.BlockSpec((1, tm // MOBA_BLOCK, 1, LANES), lambda bi, i: (bi, i, 0, 0)))
    return pl.pallas_call(
        functools.partial(_norm_mm_body, n_plain=n_plain, d_model=d, with_t=with_t, with_means=with_means,
                          with_norms=with_norms),
        grid=(b, s // tm),
        in_specs=[pl.BlockSpec((1, tm, d), lambda bi, i: (bi, i, 0)),
                  pl.BlockSpec((1, d), lambda bi, i: (0, 0)),
                  pl.BlockSpec((1, 1, d), lambda bi, i: (bi, 0, shift_idx)),
                  pl.BlockSpec((1, 1, d), lambda bi, i: (bi, 0, scale_idx)),
                  pl.BlockSpec((d, n), lambda bi, i: (0, 0))],
        out_specs=out_specs,
        out_shape=out_shape,
        compiler_params=_params("parallel", "parallel"),
        name="norm_matmul" + ("_t" if with_t else "") + ("_means" if with_means else "") + ("_norms" if with_norms else ""),
    )(x, g.reshape(1, d), mod, mod, w)


def _per_head(q2, dh, qt_ref):
    lane = lax.broadcasted_iota(jnp.int32, (q2.shape[0], LANES), 1)
    out = []
    for t in range(q2.shape[1] // LANES):
        tile = q2[:, t * LANES:(t + 1) * LANES]
        zero = jnp.zeros_like(tile)
        for half in range(LANES // dh):
            mine = (lane >= half * dh) & (lane < (half + 1) * dh)
            qt_ref[len(out)] = jnp.where(mine, tile, zero).T
            out.append(t)
    return out


def _sb_body(q_ref, k_ref, vt_ref, o_ref, acc_ref, qt_ref, *, tq, dh, group_size):
    i = pl.program_id(2)
    tiles = _per_head(q_ref[0], dh, qt_ref)
    heads = range(len(tiles))
    key = lax.broadcasted_iota(jnp.int32, (tq, tq), 0)
    qry = lax.broadcasted_iota(jnp.int32, (tq, tq), 1)
    past = key < qry
    neg_upper = jnp.where(qry > key, -1.0, 0.0).astype(BF16)
    sign_bit = jnp.uint32(0x80000000)

    acc_ref[...] = jnp.zeros_like(acc_ref)

    def blocks(js, carries, n_diag=0):
        ks = [k_ref[0, pl.ds(pl.multiple_of(j * tq, tq), tq), :] for j in js]
        vts = [vt_ref[0, :, pl.ds(pl.multiple_of(j * tq, tq), tq)] for j in js]
        chains = [(b, h) for b in range(len(js)) for h in heads]
        carry = list(carries)
        acc = [acc_ref[h] for h in heads]
        z, lbeta, sp, sp0, excl, w = {}, {}, {}, {}, {}, {}

        def stage(c, s):
            b, h = c
            if s == 0:
                t = tiles[h]
                z[c] = jnp.dot(ks[b][:, t * LANES:(t + 1) * LANES], qt_ref[h],
                               preferred_element_type=F32)
            elif s == 1:
                zc = z.pop(c)
                neg_abs = lax.bitcast_convert_type(lax.bitcast_convert_type(zc, jnp.uint32) | sign_bit, F32)
                v = jnp.maximum(zc, 0.0) + jnp.log2(1.0 + jnp.exp2(neg_abs))
                lbeta[c] = zc - v
                if b < n_diag:
                    v = jnp.where(past, v, 0.0)
                sp0[c] = v[0:1, :]
                sp[c] = v.astype(BF16)
            elif s == 2:
                excl[c] = jnp.dot(neg_upper, sp.pop(c), preferred_element_type=F32)
            elif s == 3:
                v = jnp.exp2((lbeta.pop(c) + excl[c] + carry[h]).astype(BF16))
                if b < n_diag:
                    v = jnp.where(past, v, jnp.zeros_like(v))
                w[c] = v
                carry[h] = carry[h] + excl.pop(c)[0:1, :] - sp0.pop(c)
            else:
                vt_h = vts[b][h * dh:(h + 1) * dh, :]
                acc[h] = acc[h] + jnp.dot(vt_h, w.pop(c), preferred_element_type=F32)

        n_stages = 5
        for t in range(len(chains) + n_stages - 1):
            for ci, c in enumerate(chains):
                if 0 <= t - ci < n_stages:
                    stage(c, t - ci)
        for h in heads:
            acc_ref[h] = acc[h]
        return tuple(carry)

    def alive(carries):
        return (jnp.max(functools.reduce(jnp.maximum, carries)) > DEAD_LOG2).astype(jnp.int32)

    def walk(n, tiles_of, state):
        def body(st):
            carries = blocks(tiles_of(st[0]), st[2:])
            return (st[0] + 1, alive(carries)) + carries
        st = lax.while_loop(lambda st: (st[0] < n) & (st[1] > 0), body, (jnp.int32(0),) + state)
        return st[1:]

    zeros = tuple(jnp.zeros((1, tq), F32) for _ in heads)
    carries = lax.cond(i > 0,
                       lambda: blocks([i, i - 1], zeros, n_diag=1),
                       lambda: blocks([i], zeros, n_diag=1))
    rest = jnp.maximum(i - 1, 0)
    n_single = jnp.where(rest > 0, rest - group_size * ((rest - 1) // group_size), 0)
    state = walk(n_single, lambda t: [i - 2 - t], (alive(carries),) + carries)
    top = i - 2 - n_single
    walk((rest - n_single) // group_size,
         lambda t: [top - group_size * t - g for g in range(group_size)], state)
    o_ref[0] = jnp.concatenate([acc_ref[h] for h in heads], axis=0).T.astype(o_ref.dtype)


def _sb_attention(qk, vt, d, tq=256, group_size=4, heads_per_step=4):
    b, s, _ = qk.shape
    dh = d // N_HEADS
    width = heads_per_step * dh
    assert LANES % dh == 0 and width % LANES == 0 and d % width == 0 and s % tq == 0
    nstep = d // width
    return pl.pallas_call(
        functools.partial(_sb_body, tq=tq, dh=dh, group_size=group_size),
        grid=(b, nstep, s // tq),
        in_specs=[pl.BlockSpec((1, tq, width), lambda bi, hp, i: (bi, i, hp)),
                  pl.BlockSpec((1, s, width), lambda bi, hp, i: (bi, 0, nstep + hp)),
                  pl.BlockSpec((1, width, s), lambda bi, hp, i: (bi, hp, 0))],
        out_specs=pl.BlockSpec((1, tq, width), lambda bi, hp, i: (bi, i, hp)),
        out_shape=jax.ShapeDtypeStruct((b, s, d), BF16),
        scratch_shapes=[pltpu.VMEM((heads_per_step, dh, tq), F32),
                        pltpu.VMEM((heads_per_step, LANES, tq), BF16)],
        compiler_params=_params("parallel", "parallel", "arbitrary"),
        name="sb_attention",
    )(qk, qk, vt)


def _moba_body(slopes_ref, dead_ref, q_ref, k_ref, vt_ref, km_ref, o_ref,
               acc_ref, m_ref, bias_ref, off_ref, ua_ref, ub_ref, cma_ref, cmb_ref, vta_ref, qt_ref,
               *, tq, dh, nb, topk, group_size):
    hp = pl.program_id(1)
    i = pl.program_id(2)

    @pl.when(i == 0)
    def _():
        row = lax.broadcasted_iota(jnp.int32, (BF16_SUBLANES, vta_ref.shape[2]), 0)
        tail = jnp.where(row == 0, 1.0, 0.0).astype(BF16)
        for h in range(vta_ref.shape[0]):
            vta_ref[h, :dh, :] = vt_ref[0, h * dh:(h + 1) * dh, :]
            vta_ref[h, dh:, :] = tail

    tile_of = [slice(t * LANES, (t + 1) * LANES)
               for t in _per_head(q_ref[0], dh, qt_ref)]
    heads = range(len(tile_of))
    km = km_ref[0]
    key = lax.broadcasted_iota(jnp.int32, (tq, tq), 0)
    qry = lax.broadcasted_iota(jnp.int32, (tq, tq), 1)
    rel = (qry - key).astype(F32)
    causal = qry >= key
    blk = lax.broadcasted_iota(jnp.int32, (nb, tq), 0).astype(F32)
    own = i.astype(F32)
    slope = [slopes_ref[len(tile_of) * hp + h] for h in heads]
    n_dead = dead_ref[(pl.program_id(0) * pl.num_programs(1) + hp) * nb + i]
    masked_row = jnp.full((1, tq), MASKED, F32)

    def tiles_of(g):
        return [i - group_size * g - t for t in range(group_size)]

    def logits(g, u_ref, cm_ref, with_own=False):
        for t, j in enumerate(tiles_of(g)):
            k2 = k_ref[0, pl.ds(pl.multiple_of(jnp.maximum(j, 0) * tq, tq), tq), :]
            for h in heads:
                u = jnp.dot(k2[:, tile_of[h]], qt_ref[h], preferred_element_type=F32) + bias_ref[h]
                if with_own and t == 0:
                    u = jnp.where(causal, u, MASKED)
                u_ref[h, t] = u
                cm_ref[h, t] = jnp.max(u, axis=0, keepdims=True)

    for h in heads:
        bias_ref[h] = -slope[h] * rel
        m_ref[h] = masked_row
        acc_ref[h] = jnp.zeros(acc_ref.shape[1:], F32)
    logits(0, ua_ref, cma_ref, with_own=True)

    km_hi = km.astype(BF16)
    km_mid = (km - km_hi.astype(F32)).astype(BF16)
    km_lo = (km - km_hi.astype(F32) - km_mid.astype(F32)).astype(BF16)
    for h in heads:
        gate = sum(jnp.dot(part[:, tile_of[h]], qt_ref[h], preferred_element_type=F32)
                   for part in (km_lo, km_mid, km_hi))
        gate = jnp.where(blk < own, gate, -jnp.inf)
        sel = jnp.zeros((nb, tq), F32)
        for _ in range(topk):
            mx = jnp.max(gate, axis=0, keepdims=True)
            idx = jnp.min(jnp.where(gate == mx, blk, float(nb)), axis=0, keepdims=True)
            pick = blk == idx
            sel = jnp.where(pick & (blk < own), 1.0, sel)
            gate = jnp.where(pick, -jnp.inf, gate)
        off = jnp.where((sel > 0.0) | (blk == own), -slope[h] * float(tq) * (own - blk), MASKED)
        for r in range(nb):
            off_ref[h, r] = off[r:r + 1, :]
        off_ref[h, nb] = masked_row

    def update(g, u_ref, cm_ref):
        js = tiles_of(g)
        offs = [[off_ref[h, jnp.where(j >= 0, j, nb)] for j in js] for h in heads]
        m_new, acc = [], []
        for h in heads:
            m_old = m_ref[h]
            mh = m_old
            for t in range(group_size):
                mh = jnp.maximum(mh, cm_ref[h, t] + offs[h][t])
            m_new.append(mh)
            acc.append(jnp.exp2(m_old - mh) * acc_ref[h])
        for t, j in enumerate(js):
            start = pl.multiple_of(jnp.maximum(j, 0) * tq, tq)
            for h in heads:
                p = jnp.exp2((u_ref[h, t] + (offs[h][t] - m_new[h])).astype(BF16))
                acc[h] = acc[h] + jnp.dot(vta_ref[h, :, pl.ds(start, tq)], p, preferred_element_type=F32)
        for h in heads:
            m_ref[h] = m_new[h]
            acc_ref[h] = acc[h]

    n_groups = (i - n_dead + group_size) // group_size

    def two_groups(t, carry):
        g = 2 * t
        logits(g + 1, ub_ref, cmb_ref)
        update(g, ua_ref, cma_ref)
        logits(g + 2, ua_ref, cma_ref)
        update(g + 1, ub_ref, cmb_ref)
        return carry

    lax.fori_loop(0, (n_groups + 1) // 2, two_groups, 0)

    o_t = jnp.concatenate([acc_ref[h, :dh, :] / acc_ref[h, dh:dh + 1, :] for h in heads], axis=0)
    o_ref[0] = o_t.T.astype(o_ref.dtype)


def _block_norms(n2):
    b, nb = n2.shape[:2]
    return NORM_MARGIN * jnp.sqrt(n2.reshape(b, nb, LANES)[:, :, :N_HEADS])


def _dead_blocks(q_norm, k_norm, slopes, tq):
    nb = q_norm.shape[1]
    qn = q_norm.transpose(0, 2, 1)[:, :, :, None]
    kn = k_norm.transpose(0, 2, 1)
    kp = lax.cummax(kn, axis=2)[:, :, None, :]
    i = jnp.arange(nb, dtype=F32)[:, None]
    j = jnp.arange(nb, dtype=F32)[None, :]
    nearest = tq * (i - j) - (tq - 1)
    bound = qn * kp + qn * kn[:, :, :, None] - slopes[None, :, None, None] * nearest
    dead = (bound <= DEAD_LOG2) & (j < i)
    return jnp.sum(dead, axis=-1).astype(jnp.int32)


def _moba_attention(q, k, vt, kmeans, q_norm, k_norm, slopes, group_size=2, heads_per_step=2):
    b, s, d = q.shape
    dh = d // N_HEADS
    tq = MOBA_BLOCK
    width = heads_per_step * dh
    assert LANES % dh == 0 and width % LANES == 0 and d % width == 0 and s % tq == 0
    nb = s // tq
    nstep = d // width
    nh = heads_per_step
    n_dead = _dead_blocks(q_norm, k_norm, slopes, tq).reshape(b, nstep, nh, nb).min(axis=2).reshape(-1)
    return pl.pallas_call(
        functools.partial(_moba_body, tq=tq, dh=dh, nb=nb, topk=min(MOBA_TOPK, nb), group_size=group_size),
        grid=(b, nstep, nb),
        in_specs=[pl.BlockSpec(memory_space=pltpu.SMEM),
                  pl.BlockSpec(memory_space=pltpu.SMEM),
                  pl.BlockSpec((1, tq, width), lambda bi, hp, i: (bi, i, hp)),
                  pl.BlockSpec((1, s, width), lambda bi, hp, i: (bi, 0, hp)),
                  pl.BlockSpec((1, width, s), lambda bi, hp, i: (bi, hp, 0)),
                  pl.BlockSpec((1, nb, width), lambda bi, hp, i: (bi, 0, hp))],
        out_specs=pl.BlockSpec((1, tq, width), lambda bi, hp, i: (bi, i, hp)),
        out_shape=jax.ShapeDtypeStruct((b, s, d), BF16),
        scratch_shapes=[pltpu.VMEM((nh, dh + BF16_SUBLANES, tq), F32),
                        pltpu.VMEM((nh, 1, tq), F32),
                        pltpu.VMEM((nh, tq, tq), F32),
                        pltpu.VMEM((nh, nb + 1, 1, tq), F32),
                        pltpu.VMEM((nh, group_size, tq, tq), F32),
                        pltpu.VMEM((nh, group_size, tq, tq), F32),
                        pltpu.VMEM((nh, group_size, 1, tq), F32),
                        pltpu.VMEM((nh, group_size, 1, tq), F32),
                        pltpu.VMEM((nh, dh + BF16_SUBLANES, s), BF16),
                        pltpu.VMEM((nh, LANES, tq), BF16)],
        compiler_params=_params("arbitrary", "arbitrary", "arbitrary"),
        name="moba_attention",
    )(slopes, n_dead, q, k, vt, kmeans)


def _proj_res_body(o_ref, w_ref, x_ref, gt_ref, out_ref):
    y = jnp.dot(o_ref[0], w_ref[...], preferred_element_type=F32)
    out_ref[0] = x_ref[0] + (1.0 + gt_ref[0]) * y


def _proj_residual(o, w, x, mod, gate_idx, tm=512):
    b, s, d = x.shape
    return pl.pallas_call(
        _proj_res_body,
        grid=(b, s // tm),
        in_specs=[pl.BlockSpec((1, tm, d), lambda bi, i: (bi, i, 0)),
                  pl.BlockSpec((d, d), lambda bi, i: (0, 0)),
                  pl.BlockSpec((1, tm, d), lambda bi, i: (bi, i, 0)),
                  pl.BlockSpec((1, 1, d), lambda bi, i: (bi, 0, gate_idx))],
        out_specs=pl.BlockSpec((1, tm, d), lambda bi, i: (bi, i, 0)),
        out_shape=jax.ShapeDtypeStruct((b, s, d), F32),
        compiler_params=_params("parallel", "parallel"),
        name="proj_residual",
    )(o, w, x, mod)


def _route(logits):
    tm = logits.shape[0]
    lane = lax.broadcasted_iota(jnp.int32, (tm, LANES), 1).astype(F32)
    big = float(LANES)
    is_grp = (lane >= float(N_EXPERTS)) & (lane < float(N_EXPERTS + N_GROUPS))
    lg = jnp.where(is_grp, logits, -jnp.inf)
    mg = jnp.max(lg, axis=-1, keepdims=True)
    p_g = 1.0 / jnp.sum(jnp.exp(lg - mg), axis=-1, keepdims=True)
    g_sel = jnp.min(jnp.where(lg == mg, lane, big), axis=-1, keepdims=True) - float(N_EXPERTS)
    lo = g_sel * float(EXPERTS_PER_GROUP)
    in_grp = (lane >= lo) & (lane < lo + float(EXPERTS_PER_GROUP))
    le = jnp.where(in_grp, logits, -jnp.inf)
    m1 = jnp.max(le, axis=-1, keepdims=True)
    i1 = jnp.min(jnp.where(le == m1, lane, big), axis=-1, keepdims=True)
    le2 = jnp.where(lane == i1, -jnp.inf, le)
    m2 = jnp.max(le2, axis=-1, keepdims=True)
    i2 = jnp.min(jnp.where(le2 == m2, lane, big), axis=-1, keepdims=True)
    e2 = jnp.exp(m2 - m1)
    den = 1.0 + e2
    return jnp.where(lane == i1, p_g / den, 0.0) + jnp.where(lane == i2, p_g * e2 / den, 0.0)


def _moe_body(x_ref, g_ref, sh_ref, sc_ref, gt_ref, wr_ref, br_ref, wg_ref, wu_ref, wd_ref, gf_ref,
              out_ref, h_ref, comb_ref, acc_ref, *, final_norm):
    gi = pl.program_id(1)

    @pl.when(gi == 0)
    def _():
        h = _norm_mod(x_ref[...], g_ref[...], sh_ref[0], sc_ref[0])
        h_hi = h.astype(BF16)
        h_ref[...] = h_hi
        h_lo = (h - h_hi.astype(F32)).astype(BF16)
        w = wr_ref[...]
        w_hi = w.astype(BF16)
        w_lo = (w - w_hi.astype(F32)).astype(BF16)
        logits = (jnp.dot(h_hi, w_hi, preferred_element_type=F32)
                  + (jnp.dot(h_hi, w_lo, preferred_element_type=F32)
                     + jnp.dot(h_lo, w_hi, preferred_element_type=F32))) + br_ref[...]
        comb_ref[...] = _route(logits)
        acc_ref[...] = jnp.zeros_like(acc_ref)

    h = h_ref[...]
    comb = comb_ref[...]
    lane = lax.broadcasted_iota(jnp.int32, comb.shape, 1)
    for e in range(EXPERTS_PER_GROUP):
        ce = jnp.sum(jnp.where(lane == gi * EXPERTS_PER_GROUP + e, comb, 0.0), axis=-1, keepdims=True)
        hg = jnp.dot(h, wg_ref[e], preferred_element_type=F32)
        hu = jnp.dot(h, wu_ref[e], preferred_element_type=F32)
        hid = hg * (1.0 / (1.0 + jnp.exp(-hg))) * hu * ce
        acc_ref[...] += jnp.dot(hid.astype(BF16), wd_ref[e], preferred_element_type=F32)

    @pl.when(gi == N_GROUPS - 1)
    def _():
        y = x_ref[...] + (1.0 + gt_ref[0]) * acc_ref[...]
        if final_norm:
            ms = jnp.mean(y * y, axis=-1, keepdims=True)
            y = y * lax.rsqrt(ms + RMS_EPS) * gf_ref[...]
        out_ref[...] = y


def _moe(x, g, mod, shift_idx, scale_idx, gate_idx, w_router, b_router, w_gate, w_up, w_down, g_final,
         final_norm, tm=1024):
    b, s, d = x.shape
    ff = w_gate.shape[2]
    t = b * s
    per_batch = s // tm
    epg = EXPERTS_PER_GROUP
    out = pl.pallas_call(
        functools.partial(_moe_body, final_norm=final_norm),
        grid=(t // tm, N_GROUPS),
        in_specs=[pl.BlockSpec((tm, d), lambda i, gi: (i, 0)),
                  pl.BlockSpec((1, d), lambda i, gi: (0, 0)),
                  pl.BlockSpec((1, 1, d), lambda i, gi: (i // per_batch, 0, shift_idx)),
                  pl.BlockSpec((1, 1, d), lambda i, gi: (i // per_batch, 0, scale_idx)),
                  pl.BlockSpec((1, 1, d), lambda i, gi: (i // per_batch, 0, gate_idx)),
                  pl.BlockSpec((d, LANES), lambda i, gi: (0, 0)),
                  pl.BlockSpec((1, LANES), lambda i, gi: (0, 0)),
                  pl.BlockSpec((epg, d, ff), lambda i, gi: (gi, 0, 0)),
                  pl.BlockSpec((epg, d, ff), lambda i, gi: (gi, 0, 0)),
                  pl.BlockSpec((epg, ff, d), lambda i, gi: (gi, 0, 0)),
                  pl.BlockSpec((1, d), lambda i, gi: (0, 0))],
        out_specs=pl.BlockSpec((tm, d), lambda i, gi: (i, 0)),
        out_shape=jax.ShapeDtypeStruct((t, d), F32),
        scratch_shapes=[pltpu.VMEM((tm, d), BF16),
                        pltpu.VMEM((tm, LANES), F32),
                        pltpu.VMEM((tm, d), F32)],
        compiler_params=_params("parallel", "arbitrary"),
        name="moe_final" if final_norm else "moe",
    )(x.reshape(t, d), g.reshape(1, d), mod, mod, mod, w_router, b_router, w_gate, w_up, w_down,
      g_final.reshape(1, d))
    return out.reshape(b, s, d)


def _router_params(w_rg, b_rg, w_re, b_re):
    d = w_rg.shape[0]
    pad = LANES - N_EXPERTS - N_GROUPS
    w = jnp.concatenate([w_re, w_rg, jnp.zeros((d, pad), F32)], axis=1)
    bias = jnp.concatenate([b_re, b_rg, jnp.zeros((pad,), F32)]).reshape(1, LANES)
    return w, bias


def kernel(x, c, w_ada, b_ada, g_attn, g_ffn, w_qkv_a, w_o_a, w_ada_kv, b_ada_kv, g_kv, w_kv, w_q_b, w_o_b,
           w_rg, b_rg, w_re, b_re, w_gate, w_up, w_down, g_final):
    b, s, d = x.shape
    depth = w_ada.shape[0]
    n_a = depth // 2
    q_scale = LOG2E * (d // N_HEADS) ** -0.5
    slopes = LOG2E * jnp.exp2(-ALIBI_MAX_BIAS * jnp.arange(1, N_HEADS + 1, dtype=F32) / N_HEADS)
    mods = _ada(c, w_ada, b_ada).reshape(depth, b, 1, 6 * d)
    mod_kv = _ada(c, w_ada_kv[None], b_ada_kv[None]).reshape(b, 1, 2 * d)
    k = vt = kmeans = k_norm = None
    for l in range(depth):
        mod = mods[l]
        if l < n_a:
            w_qkv = jnp.concatenate([w_qkv_a[l][:, :d] * q_scale, w_qkv_a[l][:, d:]], axis=1).astype(BF16)
            qk, v_t = _norm_matmul(x, g_attn[l], mod, 0, 1, w_qkv, with_t=True)
            o = _sb_attention(qk, v_t, d)
            w_o = w_o_a[l]
        else:
            j = l - n_a
            q, qn2 = _norm_matmul(x, g_attn[l], mod, 0, 1, (w_q_b[j] * q_scale).astype(BF16), with_norms=True)
            o = _moba_attention(q, k, vt, kmeans, _block_norms(qn2), k_norm, slopes)
            w_o = w_o_b[j]
        x = _proj_residual(o, w_o.astype(BF16), x, mod, 2)
        w_router, b_router = _router_params(w_rg[l], b_rg[l], w_re[l], b_re[l])
        x = _moe(x, g_ffn[l], mod, 3, 4, 5, w_router, b_router, w_gate[l].astype(BF16), w_up[l].astype(BF16),
                 w_down[l].astype(BF16), g_final, final_norm=(l == depth - 1))
        if l == n_a - 1:
            k, vt, kmeans, kn2 = _norm_matmul(x, g_kv, mod_kv, 0, 1, w_kv.astype(BF16), with_t=True,
                                              with_means=True, with_norms=True)
            kmeans = kmeans.reshape(b, s // MOBA_BLOCK, d)
            k_norm = _block_norms(kn2)
    return x
```
